```python
import jax, jax.numpy as jnp
from jax import lax
import numpy as np

D_MODEL = 1024
BATCH = 16
SEQ = 2048
DEPTH = 4
DEC_BATCH = 32
DEC_SEQ = 32
PAST_LEN = 2048

CHUNK = 64
BR_WIDTH = D_MODEL // 4
N_BRANCH = 5
A_BLOCK = 128
A_GROUPS = 4
A_HEAD = BR_WIDTH // A_GROUPS
B_KERNEL = 31
C_KERNEL = 3
D_WINDOWS = (2, 4, 8, 16)
D_GROUPS = 4
D_GROUP_DIM = BR_WIDTH // D_GROUPS
D_HIST = max(D_WINDOWS) - 1
N_MEM = 256
X_HEADS = 4
X_HEAD_DIM = BR_WIDTH // X_HEADS
D_FF = ((8 * D_MODEL // 3 + 255) // 256) * 256
W_IN_COLS = 9 * BR_WIDTH
EPS = 1e-6

kernel_name = "hybrid_streaming_encoder_step"


def rmsnorm(x, g):
    x32 = x.astype(jnp.float32)
    y = x32 * lax.rsqrt(jnp.mean(x32 * x32, axis=-1, keepdims=True) + EPS) * g.astype(jnp.float32)
    return y.astype(x.dtype)


def layernorm(x, g, b):
    x32 = x.astype(jnp.float32)
    mu = jnp.mean(x32, axis=-1, keepdims=True)
    var = jnp.mean(jnp.square(x32 - mu), axis=-1, keepdims=True)
    y = (x32 - mu) * lax.rsqrt(var + EPS) * g.astype(jnp.float32) + b.astype(jnp.float32)
    return y.astype(x.dtype)


def causal_dwconv(x_ext, w):
    c = x_ext.shape[-1]
    return lax.conv_general_dilated(x_ext, w[:, None, :].astype(x_ext.dtype), (1,), 'VALID',
                                    dimension_numbers=('NWC', 'WIO', 'NWC'), feature_group_count=c)


def gmlp_spatial(v, ws, bs):
    bsz, L, _ = v.shape
    pad = (-L) % A_BLOCK
    vb = jnp.pad(v, ((0, 0), (0, pad), (0, 0))).reshape(bsz, -1, A_BLOCK, A_GROUPS, A_HEAD)
    idx = jnp.arange(A_BLOCK)
    mask = (idx[None, :] // CHUNK) <= (idx[:, None] // CHUNK)
    wm = jnp.where(mask[None], ws, jnp.zeros_like(ws))
    s = jnp.einsum('gij,bnjgc->bnigc', wm, vb) + bs.T[None, None, :, :, None]
    return s.reshape(bsz, -1, A_GROUPS * A_HEAD)[:, :L]


def multiscale_pool(d_ext, pos0, d_w, d_scale):
    bsz, Lx, _ = d_ext.shape
    L = Lx - D_HIST
    cs = jnp.cumsum(d_ext.astype(jnp.float32), axis=1)
    cs0 = jnp.concatenate([jnp.zeros((bsz, 1, BR_WIDTH), jnp.float32), cs], axis=1)
    pos = pos0 + jnp.arange(L)
    outs = []
    for g, w in enumerate(D_WINDOWS):
        sl = slice(g * D_GROUP_DIM, (g + 1) * D_GROUP_DIM)
        csg = cs0[:, :, sl]
        win_sum = csg[:, D_HIST + 1:D_HIST + 1 + L] - csg[:, D_HIST + 1 - w:D_HIST + 1 - w + L]
        cnt = jnp.minimum(w, pos + 1).astype(jnp.float32)[None, :, None]
        outs.append(win_sum / cnt - d_ext[:, D_HIST:, sl].astype(jnp.float32))
    pooled = jnp.stack(outs, axis=2).astype(d_ext.dtype)
    mixed = jnp.einsum('blgc,gcd->blgd', pooled, d_w).reshape(bsz, L, BR_WIDTH)
    return mixed * d_scale


def memory_attention(zq, mem_k, mem_v):
    bsz, L, _ = zq.shape
    q = zq.reshape(bsz, L, X_HEADS, X_HEAD_DIM)
    s = jnp.einsum('blhd,bmhd->bhlm', q, mem_k).astype(jnp.float32) * (X_HEAD_DIM ** -0.5)
    p = jax.nn.softmax(s, axis=-1)
    o = jnp.einsum('bhlm,bmhd->blhd', p, mem_v.astype(jnp.float32))
    return o.reshape(bsz, L, BR_WIDTH).astype(zq.dtype)


def memory_kv(mem, mem_norm_g, w_mem_kv):
    bsz = mem.shape[0]
    kv = rmsnorm(mem, mem_norm_g) @ w_mem_kv
    k = kv[..., :BR_WIDTH].reshape(bsz, N_MEM, X_HEADS, X_HEAD_DIM)
    v = kv[..., BR_WIDTH:].reshape(bsz, N_MEM, X_HEADS, X_HEAD_DIM)
    return k, v


def trunk_layer(x, pos0, hist_b, hist_c, hist_d, mem_k, mem_v, lp):
    (norm1_g, w_in, a_ln_g, a_ln_b, a_ws, a_bs, b_conv_w, b_conv_b, b_ln_g, b_ln_b,
     c_conv_w, d_w, d_scale, w_branch, w_gate, b_gate, w_out, norm2_g,
     w_ffn_gate, w_ffn_up, w_ffn_down) = lp
    W = BR_WIDTH
    xn = rmsnorm(x, norm1_g)
    z = xn @ w_in
    za, zb, zc, zd, zq = z[..., :2 * W], z[..., 2 * W:4 * W], z[..., 4 * W:7 * W], z[..., 7 * W:8 * W], z[..., 8 * W:]
    ga = jax.nn.gelu(za)
    ua = ga[..., :W]
    va = layernorm(ga[..., W:], a_ln_g, a_ln_b)
    oa = ua * gmlp_spatial(va, a_ws, a_bs)
    b_in = zb[..., :W] * jax.nn.sigmoid(zb[..., W:])
    b_ext = jnp.concatenate([hist_b, b_in], axis=1)
    ob = jax.nn.silu(layernorm(causal_dwconv(b_ext, b_conv_w) + b_conv_b, b_ln_g, b_ln_b))
    c_b, c_c, c_x = zc[..., :W], zc[..., W:2 * W], zc[..., 2 * W:]
    c_ext = jnp.concatenate([hist_c, c_c * c_x], axis=1)
    oc = c_b * causal_dwconv(c_ext, c_conv_w)
    d_ext = jnp.concatenate([hist_d, zd], axis=1)
    od = multiscale_pool(d_ext, pos0, d_w, d_scale)
    oe = memory_attention(zq, mem_k, mem_v)
    branches = (oa, ob, oc, od, oe)
    merged = jax.nn.sigmoid(xn @ w_gate[0] + b_gate[0]) * (branches[0] @ w_branch[0])
    for n in range(1, N_BRANCH):
        merged = merged + jax.nn.sigmoid(xn @ w_gate[n] + b_gate[n]) * (branches[n] @ w_branch[n])
    x = x + merged @ w_out
    hn = rmsnorm(x, norm2_g)
    x = x + (jax.nn.silu(hn @ w_ffn_gate) * (hn @ w_ffn_up)) @ w_ffn_down
    return x, va, b_ext[:, -(B_KERNEL - 1):], c_ext[:, -(C_KERNEL - 1):], d_ext[:, -D_HIST:]


def setup_inputs(seed: int = 0) -> dict:
    key = jax.random.key(seed)
    ks = iter(jax.random.split(key, 40))
    f32 = jnp.float32

    def nrm(shape, scale):
        return jax.random.normal(next(ks), shape, f32) * scale

    W = BR_WIDTH
    return {
        "x_prompt": nrm((BATCH, SEQ, D_MODEL), 1.0),
        "x_sample": nrm((DEC_BATCH, DEC_SEQ, D_MODEL), 1.0),
        "mem_prompt": nrm((BATCH, N_MEM, D_MODEL), 1.0),
        "cache_mem_k": nrm((DEPTH, DEC_BATCH, N_MEM, X_HEADS, X_HEAD_DIM), 1.0),
        "cache_mem_v": nrm((DEPTH, DEC_BATCH, N_MEM, X_HEADS, X_HEAD_DIM), 1.0),
        "state_conv_b": nrm((DEPTH, DEC_BATCH, B_KERNEL - 1, W), 0.5),
        "state_conv_c": nrm((DEPTH, DEC_BATCH, C_KERNEL - 1, W), 0.5),
        "state_pool_d": nrm((DEPTH, DEC_BATCH, D_HIST, W), 1.0),
        "norm1_g": 1.0 + nrm((DEPTH, D_MODEL), 0.05),
        "mem_norm_g": 1.0 + nrm((DEPTH, D_MODEL), 0.05),
        "w_in": nrm((DEPTH, D_MODEL, W_IN_COLS), D_MODEL ** -0.5),
        "a_ln_g": 1.0 + nrm((DEPTH, W), 0.05),
        "a_ln_b": nrm((DEPTH, W), 0.02),
        "a_ws": nrm((DEPTH, A_GROUPS, A_BLOCK, A_BLOCK), A_BLOCK ** -0.5),
        "a_bs": 1.0 + nrm((DEPTH, A_GROUPS, A_BLOCK), 0.1),
        "b_conv_w": nrm((DEPTH, B_KERNEL, W), B_KERNEL ** -0.5),
        "b_conv_b": nrm((DEPTH, W), 0.02),
        "b_ln_g": 1.0 + nrm((DEPTH, W), 0.05),
        "b_ln_b": nrm((DEPTH, W), 0.02),
        "c_conv_w": nrm((DEPTH, C_KERNEL, W), C_KERNEL ** -0.5),
        "d_w": nrm((DEPTH, D_GROUPS, D_GROUP_DIM, D_GROUP_DIM), D_GROUP_DIM ** -0.5),
        "d_scale": 1.0 + nrm((DEPTH, W), 0.1),
        "w_mem_kv": nrm((DEPTH, D_MODEL, 2 * W), D_MODEL ** -0.5),
        "w_branch": nrm((DEPTH, N_BRANCH, W, D_MODEL), W ** -0.5),
        "w_gate": nrm((DEPTH, N_BRANCH, D_MODEL, D_MODEL), D_MODEL ** -0.5),
        "b_gate": nrm((DEPTH, N_BRANCH, D_MODEL), 0.02),
        "w_out": nrm((DEPTH, D_MODEL, D_MODEL), 0.5 * D_MODEL ** -0.5),
        "norm2_g": 1.0 + nrm((DEPTH, D_MODEL), 0.05),
        "w_ffn_gate": nrm((DEPTH, D_MODEL, D_FF), D_MODEL ** -0.5),
        "w_ffn_up": nrm((DEPTH, D_MODEL, D_FF), D_MODEL ** -0.5),
        "w_ffn_down": nrm((DEPTH, D_FF, D_MODEL), 0.5 * D_FF ** -0.5),
        "final_norm_g": 1.0 + nrm((D_MODEL,), 0.05),
    }


def reference(x_prompt, x_sample, mem_prompt, cache_mem_k, cache_mem_v, state_conv_b, state_conv_c,
              state_pool_d, norm1_g, mem_norm_g, w_in, a_ln_g, a_ln_b, a_ws, a_bs, b_conv_w, b_conv_b,
              b_ln_g, b_ln_b, c_conv_w, d_w, d_scale, w_mem_kv, w_branch, w_gate, b_gate, w_out,
              norm2_g, w_ffn_gate, w_ffn_up, w_ffn_down, final_norm_g):
    dt = x_prompt.dtype
    xp, xs = x_prompt, x_sample
    mk_p, mv_p, cb_p, cc_p, pd_p = [], [], [], [], []
    av_s, cb_s, cc_s, pd_s = [], [], [], []
    for l in range(DEPTH):
        lp = (norm1_g[l], w_in[l], a_ln_g[l], a_ln_b[l], a_ws[l], a_bs[l], b_conv_w[l], b_conv_b[l],
              b_ln_g[l], b_ln_b[l], c_conv_w[l], d_w[l], d_scale[l], w_branch[l], w_gate[l], b_gate[l],
              w_out[l], norm2_g[l], w_ffn_gate[l], w_ffn_up[l], w_ffn_down[l])
        mk, mv = memory_kv(mem_prompt, mem_norm_g[l], w_mem_kv[l])
        hb0 = jnp.zeros((BATCH, B_KERNEL - 1, BR_WIDTH), dt)
        hc0 = jnp.zeros((BATCH, C_KERNEL - 1, BR_WIDTH), dt)
        hd0 = jnp.zeros((BATCH, D_HIST, BR_WIDTH), dt)
        xp, _, hb, hc, hd = trunk_layer(xp, 0, hb0, hc0, hd0, mk, mv, lp)
        mk_p.append(mk); mv_p.append(mv); cb_p.append(hb); cc_p.append(hc); pd_p.append(hd)
        xs, va, hb, hc, hd = trunk_layer(xs, PAST_LEN, state_conv_b[l], state_conv_c[l], state_pool_d[l],
                                         cache_mem_k[l], cache_mem_v[l], lp)
        av_s.append(va); cb_s.append(hb); cc_s.append(hc); pd_s.append(hd)
    y_prompt = rmsnorm(xp, final_norm_g)
    y_sample = rmsnorm(xs, final_norm_g)
    return (y_prompt, y_sample,
            jnp.stack(mk_p), jnp.stack(mv_p), jnp.stack(cb_p), jnp.stack(cc_p), jnp.stack(pd_p),
            jnp.stack(av_s), jnp.stack(cb_s), jnp.stack(cc_s), jnp.stack(pd_s))
```

```python
import functools

import jax
import jax.numpy as jnp
from jax import lax
from jax.experimental import pallas as pl
from jax.experimental.pallas import tpu as pltpu

D_MODEL = 1024
DEPTH = 4
CHUNK = 64
PAST_LEN = 2048
BR = D_MODEL // 4
N_BRANCH = 5
A_BLOCK = 128
A_GROUPS = 4
B_KERNEL = 31
C_KERNEL = 3
D_WINDOWS = (2, 4, 8, 16)
D_HIST = max(D_WINDOWS) - 1
N_MEM = 256
X_HEADS = 4
X_HEAD_DIM = BR // X_HEADS
D_FF = ((8 * D_MODEL // 3 + 255) // 256) * 256
W_IN_COLS = 9 * BR
EPS = 1e-6

SUBLANES = 8
B_HPAD = 32
C_HPAD = 8
D_HPAD = 16
VMEM_LIMIT = 56 * 1024 * 1024

BF16 = jnp.bfloat16
F32 = jnp.float32

V256_A_LN_G, V256_A_LN_B, V256_B_CB, V256_B_LN_G, V256_B_LN_B, V256_D_SCALE, V256_C_W0 = range(7)
V1024_NORM1, V1024_BGATE0 = 0, 1


def _dot(a, b):
    return jnp.dot(a, b, preferred_element_type=F32)


def _rms(x, g):
    return x * lax.rsqrt(jnp.mean(x * x, axis=-1, keepdims=True) + EPS) * g


def _layernorm(x, g, b):
    mu = jnp.mean(x, axis=-1, keepdims=True)
    xc = x - mu
    var = jnp.mean(xc * xc, axis=-1, keepdims=True)
    return xc * lax.rsqrt(var + EPS) * g + b


def _lane_group_select(lane_group, vals):
    out = vals[-1]
    for g in range(len(vals) - 2, -1, -1):
        out = jnp.where(lane_group == g, vals[g], out)
    return out


def _mixer_kernel(*refs, n_seg, seg_len, has_state, pos0, emit_va, row_chunk):
    it = iter(refs)
    x_ref, k_ref, v_ref = next(it), next(it), next(it)
    if has_state:
        hb_ref, hc_ref, hd_ref = next(it), next(it), next(it)
    v1024_ref, w_in_ref, v256_ref, aw_ref, abias_ref = (next(it) for _ in range(5))
    bcw_ref, dwbd_ref, wbr_ref, wg_ref, wout_ref = (next(it) for _ in range(5))
    x1_ref, hbo_ref, hco_ref, hdo_ref = (next(it) for _ in range(4))
    vao_ref = next(it) if emit_va else None
    xnb_ref, z_ref, bext, cext, dext, br_ref, ua_ref, va_ref = (next(it) for _ in range(8))

    rows = n_seg * seg_len
    j = pl.program_id(1)

    def v256(r):
        return v256_ref[0, r:r + 1, :]

    x = x_ref[...]
    xnb_ref[...] = _rms(x, v1024_ref[0, V1024_NORM1:V1024_NORM1 + 1, :]).astype(BF16)
    z_ref[...] = _dot(xnb_ref[...], w_in_ref[0])

    @pl.when(j == 0)
    def _():
        if has_state:
            bext[:, 0:B_HPAD, :] = hb_ref[...]
            cext[:, 0:C_HPAD, :] = hc_ref[...]
            dext[:, 0:D_HPAD, :] = hd_ref[...]
        else:
            bext[:, 0:B_HPAD, :] = jnp.zeros((n_seg, B_HPAD, BR), F32)
            cext[:, 0:C_HPAD, :] = jnp.zeros((n_seg, C_HPAD, BR), F32)
            dext[:, 0:D_HPAD, :] = jnp.zeros((n_seg, D_HPAD, BR), F32)

    @pl.when(j > 0)
    def _():
        bext[:, 0:B_HPAD, :] = bext[:, seg_len:seg_len + B_HPAD, :]
        cext[:, 0:C_HPAD, :] = cext[:, seg_len:seg_len + C_HPAD, :]
        dext[:, 0:D_HPAD, :] = dext[:, seg_len:seg_len + D_HPAD, :]

    lane_group = lax.broadcasted_iota(jnp.int32, (1, BR), 1) // (BR // 4)
    d_window = _lane_group_select(lane_group, [jnp.full((1, BR), w, jnp.int32) for w in D_WINDOWS])

    ch = row_chunk
    for s in range(n_seg):
        for c in range(seg_len // ch):
            t0 = c * ch
            r0 = s * seg_len + t0
            rs = pl.ds(r0, ch)

            ga = jax.nn.gelu(z_ref[rs, 0:2 * BR])
            ua_ref[rs, :] = ga[:, :BR]
            va = _layernorm(ga[:, BR:], v256(V256_A_LN_G), v256(V256_A_LN_B))
            va_ref[rs, :] = va
            if emit_va:
                vao_ref[rs, :] = va

            zb = z_ref[rs, 2 * BR:4 * BR]
            b_in = zb[:, :BR] * jax.nn.sigmoid(zb[:, BR:])
            bext[s, pl.ds(B_HPAD + t0, ch), :] = b_in
            acc = jnp.broadcast_to(v256(V256_B_CB), (ch, BR))
            for k in range(B_KERNEL):
                off = B_HPAD - (B_KERNEL - 1) + k + t0
                acc = acc + bcw_ref[0, k:k + 1, :] * bext[s, pl.ds(off, ch), :]
            ob = jax.nn.silu(_layernorm(acc, v256(V256_B_LN_G), v256(V256_B_LN_B)))
            br_ref[1, rs, :] = ob.astype(BF16)

            zc = z_ref[rs, 4 * BR:7 * BR]
            cc = zc[:, BR:2 * BR] * zc[:, 2 * BR:]
            cext[s, pl.ds(C_HPAD + t0, ch), :] = cc
            conv = v256(V256_C_W0 + C_KERNEL - 1) * cc
            for k in range(C_KERNEL - 1):
                off = C_HPAD - (C_KERNEL - 1) + k + t0
                conv = conv + v256(V256_C_W0 + k) * cext[s, pl.ds(off, ch), :]
            br_ref[2, rs, :] = (zc[:, :BR] * conv).astype(BF16)

            zd = z_ref[rs, 7 * BR:8 * BR]
            dext[s, pl.ds(D_HPAD + t0, ch), :] = zd

            def back(i, s=s, t0=t0):
                return dext[s, pl.ds(D_HPAD + t0 - i, ch), :]

            sums = []
            run = zd
            nxt = 1
            for w in D_WINDOWS:
                while nxt < w:
                    run = run + back(nxt)
                    nxt += 1
                sums.append(run)
            win = _lane_group_select(lane_group, sums)
            pos1 = pos0 + j * seg_len + t0 + 1 + lax.broadcasted_iota(jnp.int32, (ch, 1), 0)
            cnt = jnp.minimum(d_window, pos1).astype(F32)
            br_ref[3, rs, :] = (win / cnt - zd).astype(BF16)

    blk_seg = min(A_BLOCK, seg_len)
    ri = lax.broadcasted_iota(jnp.int32, (A_BLOCK, A_BLOCK), 0)
    ci = lax.broadcasted_iota(jnp.int32, (A_BLOCK, A_BLOCK), 1)
    keep = (ri // blk_seg == ci // blk_seg) & ((ci % blk_seg) // CHUNK <= (ri % blk_seg) // CHUNK)
    wsp = [jnp.where(keep, aw_ref[0, g], 0.0).astype(BF16) for g in range(A_GROUPS)]
    for blk in range(rows // A_BLOCK):
        rs = pl.ds(blk * A_BLOCK, A_BLOCK)
        vb = va_ref[rs, :].astype(BF16)
        sp = _lane_group_select(lane_group, [_dot(wsp[g], vb) for g in range(A_GROUPS)])
        br_ref[0, rs, :] = (ua_ref[rs, :] * (sp + abias_ref[0])).astype(BF16)

    for s in range(n_seg):
        rs = pl.ds(s * seg_len, seg_len)
        q = z_ref[rs, 8 * BR:9 * BR] * (X_HEAD_DIM ** -0.5)
        kb = k_ref[s].astype(BF16)
        vb = v_ref[s].astype(BF16)
        o = jnp.zeros((seg_len, BR), F32)
        for h in range(X_HEADS):
            qh = jnp.where(lane_group == h, q, 0.0).astype(BF16)
            sc = lax.dot_general(qh, kb, (((1,), (1,)), ((), ())), preferred_element_type=F32)
            p = jnp.exp(sc - jnp.max(sc, axis=-1, keepdims=True))
            inv = 1.0 / jnp.sum(p, axis=-1, keepdims=True)
            o = jnp.where(lane_group == h, _dot(p.astype(BF16), vb) * inv, o)
        br_ref[4, rs, :] = o.astype(BF16)

    xnb = xnb_ref[...]
    merged = None
    for n in range(N_BRANCH):
        bn = br_ref[n]
        if n == 3:
            bn = (_dot(bn, dwbd_ref[0]) * v256(V256_D_SCALE)).astype(BF16)
        gate = jax.nn.sigmoid(_dot(xnb, wg_ref[0, n]) + v1024_ref[0, V1024_BGATE0 + n:V1024_BGATE0 + n + 1, :])
        term = gate * _dot(bn, wbr_ref[0, n])
        merged = term if merged is None else merged + term
    x1_ref[...] = x_ref[...] + _dot(merged.astype(BF16), wout_ref[0])

    hbo_ref[...] = bext[:, seg_len:seg_len + B_HPAD, :]
    hco_ref[...] = cext[:, seg_len:seg_len + C_HPAD, :]
    hdo_ref[...] = dext[:, seg_len:seg_len + D_HPAD, :]


def _resident(shape, layer):
    nd = len(shape)
    return pl.BlockSpec((1,) + tuple(shape[1:]), lambda *_: (layer,) + (0,) * (nd - 1),
                        pipeline_mode=pl.Buffered(1))


def _mixer(x, k_all, v_all, states, lw, layer, *, n_seq, seq_len, n_seg, seg_len, pos0, emit_va):
    has_state = states is not None
    rows = n_seg * seg_len
    tiles = seq_len // seg_len
    assert seq_len % seg_len == 0 and n_seq % n_seg == 0 and rows % A_BLOCK == 0
    assert tiles == 1 or n_seg == 1
    assert seg_len % A_BLOCK == 0 or A_BLOCK % seg_len == 0
    assert seg_len >= B_HPAD
    row_chunk = min(64, seg_len)
    grid = (n_seq // n_seg, tiles)
    per_layer = n_seq // n_seg

    in_specs = [
        pl.BlockSpec((rows, D_MODEL), lambda b, j: (b * tiles + j, 0)),
        pl.BlockSpec((n_seg, N_MEM, BR), lambda b, j: (layer * per_layer + b, 0, 0)),
        pl.BlockSpec((n_seg, N_MEM, BR), lambda b, j: (layer * per_layer + b, 0, 0)),
    ]
    args = [x, k_all, v_all]
    if has_state:
        for st, hpad in zip(states, (B_HPAD, C_HPAD, D_HPAD)):
            in_specs.append(pl.BlockSpec((n_seg, hpad, BR), lambda b, j: (layer * per_layer + b, 0, 0)))
            args.append(st)
    names = ("v1024", "w_in", "v256", "a_w", "a_bias", "b_conv_w", "d_w_bd", "w_branch", "w_gate", "w_out")
    for name in names:
        in_specs.append(_resident(lw[name].shape, layer))
        args.append(lw[name])

    out_shape = [
        jax.ShapeDtypeStruct((n_seq * seq_len, D_MODEL), F32),
        jax.ShapeDtypeStruct((n_seq, B_HPAD, BR), F32),
        jax.ShapeDtypeStruct((n_seq, C_HPAD, BR), F32),
        jax.ShapeDtypeStruct((n_seq, D_HPAD, BR), F32),
    ]
    out_specs = [
        pl.BlockSpec((rows, D_MODEL), lambda b, j: (b * tiles + j, 0)),
        pl.BlockSpec((n_seg, B_HPAD, BR), lambda b, j: (b, 0, 0)),
        pl.BlockSpec((n_seg, C_HPAD, BR), lambda b, j: (b, 0, 0)),
        pl.BlockSpec((n_seg, D_HPAD, BR), lambda b, j: (b, 0, 0)),
    ]
    if emit_va:
        out_shape.append(jax.ShapeDtypeStruct((n_seq * seq_len, BR), F32))
        out_specs.append(pl.BlockSpec((rows, BR), lambda b, j: (b * tiles + j, 0)))

    scratch = [
        pltpu.VMEM((rows, D_MODEL), BF16),
        pltpu.VMEM((rows, W_IN_COLS), F32),
        pltpu.VMEM((n_seg, B_HPAD + seg_len, BR), F32),
        pltpu.VMEM((n_seg, C_HPAD + seg_len, BR), F32),
        pltpu.VMEM((n_seg, D_HPAD + seg_len, BR), F32),
        pltpu.VMEM((N_BRANCH, rows, BR), BF16),
        pltpu.VMEM((rows, BR), F32),
        pltpu.VMEM((rows, BR), F32),
    ]
    kern = functools.partial(_mixer_kernel, n_seg=n_seg, seg_len=seg_len, has_state=has_state,
                             pos0=pos0, emit_va=emit_va, row_chunk=row_chunk)
    return pl.pallas_call(
        kern, grid=grid, in_specs=in_specs, out_specs=out_specs, out_shape=out_shape,
        scratch_shapes=scratch, name="mixer_state" if has_state else "mixer_prompt",
        compiler_params=pltpu.CompilerParams(dimension_semantics=("arbitrary", "arbitrary"),
                                             vmem_limit_bytes=VMEM_LIMIT),
    )(*args)


def _ffn_kernel(x_ref, g_ref, wg_ref, wu_ref, wd_ref, gf_ref, o_ref, *, final_norm):
    x = x_ref[...]
    hn = _rms(x, g_ref[0]).astype(BF16)
    act = (jax.nn.silu(_dot(hn, wg_ref[0])) * _dot(hn, wu_ref[0])).astype(BF16)
    y = x + _dot(act, wd_ref[0])
    if final_norm:
        y = _rms(y, gf_ref[...])
    o_ref[...] = y


def _ffn(x, lw, layer, final_g, *, row_tile, final_norm):
    n_rows = x.shape[0]
    assert n_rows % row_tile == 0
    names = ("norm2_g", "w_ffn_gate", "w_ffn_up", "w_ffn_down")
    in_specs = [pl.BlockSpec((row_tile, D_MODEL), lambda i: (i, 0))]
    in_specs += [_resident(lw[n].shape, layer) for n in names]
    in_specs.append(pl.BlockSpec((1, D_MODEL), lambda i: (0, 0)))
    return pl.pallas_call(
        functools.partial(_ffn_kernel, final_norm=final_norm),
        grid=(n_rows // row_tile,), in_specs=in_specs,
        out_specs=pl.BlockSpec((row_tile, D_MODEL), lambda i: (i, 0)),
        out_shape=jax.ShapeDtypeStruct((n_rows, D_MODEL), F32), name="ffn",
        compiler_params=pltpu.CompilerParams(dimension_semantics=("arbitrary",),
                                             vmem_limit_bytes=VMEM_LIMIT),
    )(x, *[lw[n] for n in names], final_g)


def _memkv_kernel(m_ref, g_ref, w_ref, k_ref, v_ref):
    kv = _dot(_rms(m_ref[...], g_ref[0]).astype(BF16), w_ref[0])
    k_ref[0] = kv[:, :BR]
    v_ref[0] = kv[:, BR:]


def _memory_kv(mem_rows, mem_norm_g, w_mem_kv, *, row_tile):
    n_rows = mem_rows.shape[0]
    assert n_rows % row_tile == 0
    tiles = n_rows // row_tile
    out = jax.ShapeDtypeStruct((DEPTH, n_rows, BR), F32)
    return pl.pallas_call(
        _memkv_kernel, grid=(DEPTH, tiles),
        in_specs=[pl.BlockSpec((row_tile, D_MODEL), lambda l, i: (i, 0)),
                  pl.BlockSpec((1, 1, D_MODEL), lambda l, i: (l, 0, 0)),
                  pl.BlockSpec((1, D_MODEL, 2 * BR), lambda l, i: (l, 0, 0))],
        out_specs=[pl.BlockSpec((1, row_tile, BR), lambda l, i: (l, i, 0))] * 2,
        out_shape=[out, out], name="memory_kv",
        compiler_params=pltpu.CompilerParams(dimension_semantics=("arbitrary", "arbitrary"),
                                             vmem_limit_bytes=VMEM_LIMIT),
    )(mem_rows, mem_norm_g, w_mem_kv)


def _pad_rows_front(a, total):
    pad = total - a.shape[-2]
    return jnp.pad(a, [(0, 0)] * (a.ndim - 2) + [(pad, 0), (0, 0)])


def _spatial_layout(a_ws, a_bs, blk_seg):
    reps = A_BLOCK // blk_seg
    w = jnp.tile(a_ws[:, :, :blk_seg, :blk_seg], (1, 1, reps, reps))
    bias = jnp.repeat(jnp.swapaxes(a_bs, 1, 2), BR // A_GROUPS, axis=2)
    bias = jnp.tile(bias[:, :blk_seg, :], (1, reps, 1))
    return w, bias


def kernel(x_prompt, x_sample, mem_prompt, cache_mem_k, cache_mem_v, state_conv_b, state_conv_c, state_pool_d, norm1_g, mem_norm_g, w_in, a_ln_g, a_ln_b, a_ws, a_bs, b_conv_w, b_conv_b, b_ln_g, b_ln_b, c_conv_w, d_w, d_scale, w_mem_kv, w_branch, w_gate, b_gate, w_out, norm2_g, w_ffn_gate, w_ffn_up, w_ffn_down, final_norm_g):
    batch, seq, _ = x_prompt.shape
    dec_batch, dec_seq, _ = x_sample.shape
    depth = w_in.shape[0]
    assert depth == DEPTH

    zeros256 = jnp.zeros((depth, 16 - 6 - C_KERNEL, BR), F32)
    v256 = jnp.concatenate([a_ln_g[:, None], a_ln_b[:, None], b_conv_b[:, None], b_ln_g[:, None],
                            b_ln_b[:, None], d_scale[:, None], c_conv_w, zeros256], axis=1)
    v1024 = jnp.concatenate([norm1_g[:, None], b_gate, jnp.zeros((depth, 2, D_MODEL), F32)], axis=1)
    eye = jnp.eye(len(D_WINDOWS), dtype=F32)
    d_w_bd = jnp.einsum('lgcd,gh->lgchd', d_w, eye).reshape(depth, BR, BR).astype(BF16)
    lw = {
        "v1024": v1024, "v256": v256, "w_in": w_in.astype(BF16),
        "b_conv_w": b_conv_w,
        "d_w_bd": d_w_bd, "w_branch": w_branch.astype(BF16), "w_gate": w_gate.astype(BF16),
        "w_out": w_out.astype(BF16), "norm2_g": norm2_g[:, None],
        "w_ffn_gate": w_ffn_gate.astype(BF16), "w_ffn_up": w_ffn_up.astype(BF16),
        "w_ffn_down": w_ffn_down.astype(BF16),
    }
    lw_p = dict(lw)
    lw_p["a_w"], lw_p["a_bias"] = _spatial_layout(a_ws, a_bs, min(A_BLOCK, seq))
    lw_s = dict(lw)
    lw_s["a_w"], lw_s["a_bias"] = _spatial_layout(a_ws, a_bs, min(A_BLOCK, dec_seq))
    final_g = final_norm_g[None]

    mk_p, mv_p = _memory_kv(mem_prompt.reshape(batch * N_MEM, D_MODEL), mem_norm_g[:, None],
                            w_mem_kv.astype(BF16), row_tile=1024)
    k_p = mk_p.reshape(depth * batch, N_MEM, BR)
    v_p = mv_p.reshape(depth * batch, N_MEM, BR)
    k_s = cache_mem_k.reshape(depth * dec_batch, N_MEM, BR)
    v_s = cache_mem_v.reshape(depth * dec_batch, N_MEM, BR)
    states = (_pad_rows_front(state_conv_b, B_HPAD).reshape(depth * dec_batch, B_HPAD, BR),
              _pad_rows_front(state_conv_c, C_HPAD).reshape(depth * dec_batch, C_HPAD, BR),
              _pad_rows_front(state_pool_d, D_HPAD).reshape(depth * dec_batch, D_HPAD, BR))

    xp = x_prompt.reshape(batch * seq, D_MODEL)
    xs = x_sample.reshape(dec_batch * dec_seq, D_MODEL)
    cb_p, cc_p, pd_p, av_s, cb_s, cc_s, pd_s = [], [], [], [], [], [], []
    for l in range(depth):
        last = l == depth - 1
        xp, hb, hc, hd = _mixer(xp, k_p, v_p, None, lw_p, l, n_seq=batch, seq_len=seq, n_seg=1,
                                seg_len=512, pos0=0, emit_va=False)
        xp = _ffn(xp, lw, l, final_g, row_tile=512, final_norm=last)
        cb_p.append(hb); cc_p.append(hc); pd_p.append(hd)
        xs, hb, hc, hd, va = _mixer(xs, k_s, v_s, states, lw_s, l, n_seq=dec_batch, seq_len=dec_seq,
                                    n_seg=8, seg_len=dec_seq, pos0=PAST_LEN, emit_va=True)
        xs = _ffn(xs, lw, l, final_g, row_tile=512, final_norm=last)
        cb_s.append(hb); cc_s.append(hc); pd_s.append(hd)
        av_s.append(va.reshape(dec_batch, dec_seq, BR))

    def tails(parts, hpad, hist):
        return jnp.stack(parts)[:, :, hpad - hist:, :]

    kv_shape = (depth, batch, N_MEM, X_HEADS, X_HEAD_DIM)
    return (xp.reshape(batch, seq, D_MODEL), xs.reshape(dec_batch, dec_seq, D_MODEL),
            mk_p.reshape(kv_shape), mv_p.reshape(kv_shape),
            tails(cb_p, B_HPAD, B_KERNEL - 1), tails(cc_p, C_HPAD, C_KERNEL - 1), tails(pd_p, D_HPAD, D_HIST),
            jnp.stack(av_s),
            tails(cb_s, B_HPAD, B_KERNEL - 1), tails(cc_s, C_HPAD, C_KERNEL - 1), tails(pd_s, D_HPAD, D_HIST))
```

```python
import functools

import jax
import jax.numpy as jnp
from jax import lax
from jax.experimental import pallas as pl
from jax.experimental.pallas import tpu as pltpu

D_MODEL = 1024
DEPTH = 4
CHUNK = 64
PAST_LEN = 2048
BR = D_MODEL // 4
N_BRANCH = 5
A_BLOCK = 128
A_GROUPS = 4
B_KERNEL = 31
C_KERNEL = 3
D_WINDOWS = (2, 4, 8, 16)
D_HIST = max(D_WINDOWS) - 1
N_MEM = 256
X_HEADS = 4
X_HEAD_DIM = BR // X_HEADS
D_FF = ((8 * D_MODEL // 3 + 255) // 256) * 256
W_IN_COLS = 9 * BR
EPS = 1e-6

SUBLANES = 8
LANES = 128
B_HPAD = 32
C_HPAD = 8
D_HPAD = 16
VMEM_LIMIT = 56 * 1024 * 1024
GATE_COLS = 256
MIXER_SUBTILES = 2

BF16 = jnp.bfloat16
F32 = jnp.float32

V256_A_LN_G, V256_A_LN_B, V256_B_CB, V256_B_LN_G, V256_B_LN_B, V256_D_SCALE, V256_C_W0 = range(7)
V1024_NORM1, V1024_BGATE0 = 0, 1


def _dot(a, b):
    return jnp.dot(a, b, preferred_element_type=F32)


def _sigmoid(x):
    return 0.5 * jnp.tanh(0.5 * x) + 0.5


def _rms(x, g):
    return x * lax.rsqrt(jnp.mean(x * x, axis=-1, keepdims=True) + EPS) * g


def _layernorm(x, g, b):
    mu = jnp.mean(x, axis=-1, keepdims=True)
    xc = x - mu
    var = jnp.mean(xc * xc, axis=-1, keepdims=True)
    return xc * lax.rsqrt(var + EPS) * g + b


def _lane_group_select(lane_group, vals):
    out = vals[-1]
    for g in range(len(vals) - 2, -1, -1):
        out = jnp.where(lane_group == g, vals[g], out)
    return out


def _shift_up(x, n):
    return x if n == 0 else pltpu.roll(x, x.shape[0] - n, axis=0)


def _shift_down(x, n):
    return x if n == 0 else pltpu.roll(x, n, axis=0)


def _causal_conv31(prev, cur, w_ref, bias):
    ch = cur.shape[0]
    ext = jnp.concatenate([prev, cur], axis=0)
    lead = B_HPAD - (B_KERNEL - 1)
    acc = bias + w_ref(B_KERNEL - 1) * cur
    for b in range(SUBLANES):
        shifted = _shift_up(ext, b)
        for a in range(B_HPAD // SUBLANES):
            k = a * SUBLANES + b - lead
            if 0 <= k < B_KERNEL - 1:
                acc = acc + w_ref(k) * shifted[a * SUBLANES:a * SUBLANES + ch]
    return acc


def _mixer_kernel(*refs, n_seg, seg_len, has_state, pos0, emit_va, row_chunk, n_sub):
    it = iter(refs)
    x_ref, k_ref, v_ref = next(it), next(it), next(it)
    if has_state:
        hb_ref, hc_ref, hd_ref = next(it), next(it), next(it)
    v1024_ref, w_in_ref, v256_ref, aw_ref, abias_ref = (next(it) for _ in range(5))
    bcw_ref, dwbd_ref, wbr_ref, wg_ref, wout_ref = (next(it) for _ in range(5))
    x1_ref, hbo_ref, hco_ref, hdo_ref = (next(it) for _ in range(4))
    vao_ref = next(it) if emit_va else None
    xnb_ref, z_ref, hist_b, hist_c, hist_d, br_ref, ua_ref, va_ref, gate_ref = (next(it) for _ in range(9))

    rows = n_seg * seg_len
    j = pl.program_id(1)

    def v256(r, lanes=slice(None)):
        return v256_ref[0, r:r + 1, lanes]

    @pl.when(j == 0)
    def _():
        if has_state:
            hist_b[...] = hb_ref[...]
            hist_c[...] = hc_ref[...]
            hist_d[...] = hd_ref[...]
        else:
            hist_b[...] = jnp.zeros(hist_b.shape, F32)
            hist_c[...] = jnp.zeros(hist_c.shape, F32)
            hist_d[...] = jnp.zeros(hist_d.shape, F32)

    lane_group = lax.broadcasted_iota(jnp.int32, (1, BR), 1) // (BR // 4)
    d_window = _lane_group_select(lane_group, [jnp.full((1, BR), w, jnp.int32) for w in D_WINDOWS])
    half_lo = lax.broadcasted_iota(jnp.int32, (1, LANES), 1) < LANES // 2

    ch = row_chunk

    def chunk_task(s, c, prev_b, prev_c, prev_d):
        t0 = c * ch
        rs = pl.ds(s * seg_len + t0, ch)

        ga = jax.nn.gelu(z_ref[rs, 0:2 * BR])
        ua_ref[rs, :] = ga[:, :BR]
        va = _layernorm(ga[:, BR:], v256(V256_A_LN_G), v256(V256_A_LN_B))
        va_ref[rs, :] = va
        if emit_va:
            vao_ref[rs, :] = va

        zb = z_ref[rs, 2 * BR:4 * BR]
        b_in = zb[:, :BR] * _sigmoid(zb[:, BR:])
        conv = []
        for h in range(BR // LANES):
            ls = slice(h * LANES, (h + 1) * LANES)
            conv.append(_causal_conv31(prev_b[:, ls], b_in[:, ls],
                                       lambda k, ls=ls: bcw_ref[0, k:k + 1, ls], v256(V256_B_CB, ls)))
        lnb = _layernorm(jnp.concatenate(conv, axis=1), v256(V256_B_LN_G), v256(V256_B_LN_B))
        br_ref[1, rs, :] = (lnb * _sigmoid(lnb)).astype(BF16)
        prev_b = jnp.concatenate([prev_b, b_in], axis=0)[-B_HPAD:]

        zc = z_ref[rs, 4 * BR:7 * BR]
        cc = zc[:, BR:2 * BR] * zc[:, 2 * BR:]
        ext_c = jnp.concatenate([prev_c, cc], axis=0)
        conv_c = v256(V256_C_W0 + C_KERNEL - 1) * cc
        for k in range(C_KERNEL - 1):
            off = C_HPAD - (C_KERNEL - 1) + k
            conv_c = conv_c + v256(V256_C_W0 + k) * _shift_up(ext_c, off)[:ch]
        br_ref[2, rs, :] = (zc[:, :BR] * conv_c).astype(BF16)
        prev_c = ext_c[-C_HPAD:]

        zd = z_ref[rs, 7 * BR:8 * BR]
        ext_d = jnp.concatenate([prev_d, zd], axis=0)
        wins = []
        for h in range(BR // LANES):
            e = ext_d[:, h * LANES:(h + 1) * LANES]
            s2 = e + _shift_down(e, 1)
            s4 = s2 + _shift_down(s2, 2)
            if h == 0:
                lo, hi = s2[D_HPAD:], s4[D_HPAD:]
            else:
                s8 = s4 + _shift_down(s4, 4)
                lo, hi = s8[D_HPAD:], s8[D_HPAD:] + s8[D_HPAD - 8:-8]
            wins.append(jnp.where(half_lo, lo, hi))
        win = jnp.concatenate(wins, axis=1)
        pos1 = pos0 + j * seg_len + t0 + 1 + lax.broadcasted_iota(jnp.int32, (ch, 1), 0)
        cnt = jnp.minimum(d_window, pos1).astype(F32)
        br_ref[3, rs, :] = (win / cnt - zd).astype(BF16)
        return prev_b, prev_c, ext_d[-D_HPAD:]

    def gate_task(n, cb, rs):
        cols = slice(cb * GATE_COLS, (cb + 1) * GATE_COLS)
        pre = _dot(xnb_ref[rs, :], wg_ref[0, n, :, cols])
        gate_ref[n, rs, cols] = _sigmoid(pre + v1024_ref[0, V1024_BGATE0 + n:V1024_BGATE0 + n + 1, cols])

    blk_seg = min(A_BLOCK, seg_len)
    ri = lax.broadcasted_iota(jnp.int32, (A_BLOCK, A_BLOCK), 0)
    ci = lax.broadcasted_iota(jnp.int32, (A_BLOCK, A_BLOCK), 1)
    keep = (ri // blk_seg == ci // blk_seg) & ((ci % blk_seg) // CHUNK <= (ri % blk_seg) // CHUNK)
    wsp = [jnp.where(keep, aw_ref[0, g], 0.0).astype(BF16) for g in range(A_GROUPS)]

    def spatial_task(blk):
        rs = pl.ds(blk * A_BLOCK, A_BLOCK)
        vb = va_ref[rs, :].astype(BF16)
        sp = _lane_group_select(lane_group, [_dot(wsp[g], vb) for g in range(A_GROUPS)])
        br_ref[0, rs, :] = (ua_ref[rs, :] * (sp + abias_ref[0])).astype(BF16)

    attn_acc = {}

    def attn_task(s, h, rs):
        q = z_ref[rs, 8 * BR:9 * BR] * (X_HEAD_DIM ** -0.5)
        qh = jnp.where(lane_group == h, q, 0.0).astype(BF16)
        sc = lax.dot_general(qh, k_ref[s].astype(BF16), (((1,), (1,)), ((), ())), preferred_element_type=F32)
        p = jnp.exp(sc - jnp.max(sc, axis=-1, keepdims=True))
        inv = 1.0 / jnp.sum(p, axis=-1, keepdims=True)
        oh = _dot(p.astype(BF16), v_ref[s].astype(BF16)) * inv
        attn_acc[s] = oh if h == 0 else jnp.where(lane_group == h, oh, attn_acc[s])
        if h == X_HEADS - 1:
            br_ref[4, rs, :] = attn_acc.pop(s).astype(BF16)

    sub_rows = rows // n_sub
    next_blk = 0
    hist = None
    for sub in range(n_sub):
        rsub = pl.ds(sub * sub_rows, sub_rows)

        xnb_ref[rsub, :] = _rms(x_ref[rsub, :], v1024_ref[0, V1024_NORM1:V1024_NORM1 + 1, :]).astype(BF16)
        z_ref[rsub, :] = _dot(xnb_ref[rsub, :], w_in_ref[0])

        if n_seg == 1:
            chunk_ids = [(0, c) for c in range(sub * sub_rows // ch, (sub + 1) * sub_rows // ch)]
            attn_ids = [(0, h, rsub) for h in range(X_HEADS)]
        else:
            chunk_ids = [(s, c) for s in range(n_seg) for c in range(seg_len // ch)]
            attn_ids = [(s, h, pl.ds(s * seg_len, seg_len)) for s in range(n_seg) for h in range(X_HEADS)]
        gate_ids = [(n, cb, rsub) for n in range(N_BRANCH) for cb in range(D_MODEL // GATE_COLS)]
        slots = len(chunk_ids)
        for i, (s, c) in enumerate(chunk_ids):
            if c == 0:
                hist = (hist_b[s], hist_c[s], hist_d[s])
            hist = chunk_task(s, c, *hist)
            if c == seg_len // ch - 1:
                hist_b[s], hist_c[s], hist_d[s] = hist
            for ids in gate_ids[i * len(gate_ids) // slots:(i + 1) * len(gate_ids) // slots]:
                gate_task(*ids)
            while (next_blk + 1) * A_BLOCK <= s * seg_len + (c + 1) * ch:
                spatial_task(next_blk)
                next_blk += 1
            for ids in attn_ids[i * len(attn_ids) // slots:(i + 1) * len(attn_ids) // slots]:
                attn_task(*ids)

        merged = None
        for n in range(N_BRANCH):
            bn = br_ref[n, rsub, :]
            if n == 3:
                bn = (_dot(bn, dwbd_ref[0]) * v256(V256_D_SCALE)).astype(BF16)
            term = gate_ref[n, rsub, :] * _dot(bn, wbr_ref[0, n])
            merged = term if merged is None else merged + term
        x1_ref[rsub, :] = x_ref[rsub, :] + _dot(merged.astype(BF16), wout_ref[0])

    hbo_ref[...] = hist_b[...]
    hco_ref[...] = hist_c[...]
    hdo_ref[...] = hist_d[...]


def _resident(shape, layer):
    nd = len(shape)
    return pl.BlockSpec((1,) + tuple(shape[1:]), lambda *_: (layer,) + (0,) * (nd - 1),
                        pipeline_mode=pl.Buffered(1))


def _mixer(x, k_all, v_all, states, lw, layer, *, n_seq, seq_len, n_seg, seg_len, pos0, emit_va):
    has_state = states is not None
    rows = n_seg * seg_len
    tiles = seq_len // seg_len
    assert seq_len % seg_len == 0 and n_seq % n_seg == 0 and rows % A_BLOCK == 0
    assert tiles == 1 or n_seg == 1
    assert seg_len % A_BLOCK == 0 or A_BLOCK % seg_len == 0
    assert seg_len >= B_HPAD
    row_chunk = min(64, seg_len)
    n_sub = MIXER_SUBTILES if n_seg == 1 and seg_len % (MIXER_SUBTILES * A_BLOCK) == 0 else 1
    grid = (n_seq // n_seg, tiles)
    per_layer = n_seq // n_seg

    in_specs = [
        pl.BlockSpec((rows, D_MODEL), lambda b, j: (b * tiles + j, 0)),
        pl.BlockSpec((n_seg, N_MEM, BR), lambda b, j: (layer * per_layer + b, 0, 0)),
        pl.BlockSpec((n_seg, N_MEM, BR), lambda b, j: (layer * per_layer + b, 0, 0)),
    ]
    args = [x, k_all, v_all]
    if has_state:
        for st, hpad in zip(states, (B_HPAD, C_HPAD, D_HPAD)):
            in_specs.append(pl.BlockSpec((n_seg, hpad, BR), lambda b, j: (layer * per_layer + b, 0, 0)))
            args.append(st)
    names = ("v1024", "w_in", "v256", "a_w", "a_bias", "b_conv_w", "d_w_bd", "w_branch", "w_gate", "w_out")
    for name in names:
        in_specs.append(_resident(lw[name].shape, layer))
        args.append(lw[name])

    out_shape = [
        jax.ShapeDtypeStruct((n_seq * seq_len, D_MODEL), F32),
        jax.ShapeDtypeStruct((n_seq, B_HPAD, BR), F32),
        jax.ShapeDtypeStruct((n_seq, C_HPAD, BR), F32),
        jax.ShapeDtypeStruct((n_seq, D_HPAD, BR), F32),
    ]
    out_specs = [
        pl.BlockSpec((rows, D_MODEL), lambda b, j: (b * tiles + j, 0)),
        pl.BlockSpec((n_seg, B_HPAD, BR), lambda b, j: (b, 0, 0)),
        pl.BlockSpec((n_seg, C_HPAD, BR), lambda b, j: (b, 0, 0)),
        pl.BlockSpec((n_seg, D_HPAD, BR), lambda b, j: (b, 0, 0)),
    ]
    if emit_va:
        out_shape.append(jax.ShapeDtypeStruct((n_seq * seq_len, BR), F32))
        out_specs.append(pl.BlockSpec((rows, BR), lambda b, j: (b * tiles + j, 0)))

    scratch = [
        pltpu.VMEM((rows, D_MODEL), BF16),
        pltpu.VMEM((rows, W_IN_COLS), F32),
        pltpu.VMEM((n_seg, B_HPAD, BR), F32),
        pltpu.VMEM((n_seg, C_HPAD, BR), F32),
        pltpu.VMEM((n_seg, D_HPAD, BR), F32),
        pltpu.VMEM((N_BRANCH, rows, BR), BF16),
        pltpu.VMEM((rows, BR), F32),
        pltpu.VMEM((rows, BR), F32),
        pltpu.VMEM((N_BRANCH, rows, D_MODEL), F32),
    ]
    kern = functools.partial(_mixer_kernel, n_seg=n_seg, seg_len=seg_len, has_state=has_state,
                             pos0=pos0, emit_va=emit_va, row_chunk=row_chunk, n_sub=n_sub)
    return pl.pallas_call(
        kern, grid=grid, in_specs=in_specs, out_specs=out_specs, out_shape=out_shape,
        scratch_shapes=scratch, name="mixer_state" if has_state else "mixer_prompt",
        compiler_params=pltpu.CompilerParams(dimension_semantics=("arbitrary", "arbitrary"),
                                             vmem_limit_bytes=VMEM_LIMIT),
    )(*args)


def _ffn_kernel(x_ref, g_ref, wg_ref, wu_ref, wd_ref, gf_ref, o_ref, *, final_norm):
    x = x_ref[...]
    hn = _rms(x, g_ref[0]).astype(BF16)
    hg = _dot(hn, wg_ref[0])
    act = (hg * _sigmoid(hg) * _dot(hn, wu_ref[0])).astype(BF16)
    y = x + _dot(act, wd_ref[0])
    if final_norm:
        y = _rms(y, gf_ref[...])
    o_ref[...] = y


def _ffn(x, lw, layer, final_g, *, row_tile, final_norm):
    n_rows = x.shape[0]
    assert n_rows % row_tile == 0
    names = ("norm2_g", "w_ffn_gate", "w_ffn_up", "w_ffn_down")
    in_specs = [pl.BlockSpec((row_tile, D_MODEL), lambda i: (i, 0))]
    in_specs += [_resident(lw[n].shape, layer) for n in names]
    in_specs.append(pl.BlockSpec((1, D_MODEL), lambda i: (0, 0)))
    return pl.pallas_call(
        functools.partial(_ffn_kernel, final_norm=final_norm),
        grid=(n_rows // row_tile,), in_specs=in_specs,
        out_specs=pl.BlockSpec((row_tile, D_MODEL), lambda i: (i, 0)),
        out_shape=jax.ShapeDtypeStruct((n_rows, D_MODEL), F32), name="ffn",
        compiler_params=pltpu.CompilerParams(dimension_semantics=("arbitrary",),
                                             vmem_limit_bytes=VMEM_LIMIT),
    )(x, *[lw[n] for n in names], final_g)


def _memkv_kernel(m_ref, g_ref, w_ref, k_ref, v_ref):
    kv = _dot(_rms(m_ref[...], g_ref[0]).astype(BF16), w_ref[0])
    k_ref[0] = kv[:, :BR]
    v_ref[0] = kv[:, BR:]


def _memory_kv(mem_rows, mem_norm_g, w_mem_kv, *, row_tile):
    n_rows = mem_rows.shape[0]
    assert n_rows % row_tile == 0
    tiles = n_rows // row_tile
    out = jax.ShapeDtypeStruct((DEPTH, n_rows, BR), F32)
    return pl.pallas_call(
        _memkv_kernel, grid=(DEPTH, tiles),
        in_specs=[pl.BlockSpec((row_tile, D_MODEL), lambda l, i: (i, 0)),
                  pl.BlockSpec((1, 1, D_MODEL), lambda l, i: (l, 0, 0)),
                  pl.BlockSpec((1, D_MODEL, 2 * BR), lambda l, i: (l, 0, 0))],
        out_specs=[pl.BlockSpec((1, row_tile, BR), lambda l, i: (l, i, 0))] * 2,
        out_shape=[out, out], name="memory_kv",
        compiler_params=pltpu.CompilerParams(dimension_semantics=("arbitrary", "arbitrary"),
                                             vmem_limit_bytes=VMEM_LIMIT),
    )(mem_rows, mem_norm_g, w_mem_kv)


MIXER_ROWS = 512
FFN_ROWS = 512


def _mixer_prompt_call(x, k_all, v_all, lw, layer, batch, seq):
    return _mixer(x, k_all, v_all, None, lw, layer, n_seq=batch, seq_len=seq, n_seg=1,
                  seg_len=min(MIXER_ROWS, seq), pos0=0, emit_va=False)


def _mixer_state_call(x, k_all, v_all, states, lw, layer, dec_batch, dec_seq):
    return _mixer(x, k_all, v_all, states, lw, layer, n_seq=dec_batch, seq_len=dec_seq,
                  n_seg=8, seg_len=dec_seq, pos0=PAST_LEN, emit_va=True)


def _ffn_call(x, lw, layer, final_g, final_norm):
    return _ffn(x, lw, layer, final_g, row_tile=FFN_ROWS, final_norm=final_norm)


def _pad_rows_front(a, total):
    pad = total - a.shape[-2]
    return jnp.pad(a, [(0, 0)] * (a.ndim - 2) + [(pad, 0), (0, 0)])


def _spatial_layout(a_ws, a_bs, blk_seg):
    reps = A_BLOCK // blk_seg
    w = jnp.tile(a_ws[:, :, :blk_seg, :blk_seg], (1, 1, reps, reps))
    bias = jnp.repeat(jnp.swapaxes(a_bs, 1, 2), BR // A_GROUPS, axis=2)
    bias = jnp.tile(bias[:, :blk_seg, :], (1, reps, 1))
    return w, bias


def _layout_states(state_conv_b, state_conv_c, state_pool_d):
    return tuple(_pad_rows_front(st, hpad).reshape(-1, hpad, BR)
                 for st, hpad in ((state_conv_b, B_HPAD), (state_conv_c, C_HPAD), (state_pool_d, D_HPAD)))


def _layout_params(seq, dec_seq, norm1_g, w_in, a_ln_g, a_ln_b, a_ws, a_bs, b_conv_w, b_conv_b, b_ln_g,
                   b_ln_b, c_conv_w, d_w, d_scale, w_branch, w_gate, b_gate, w_out, norm2_g, w_ffn_gate,
                   w_ffn_up, w_ffn_down):
    depth = w_in.shape[0]
    zeros256 = jnp.zeros((depth, 16 - 6 - C_KERNEL, BR), F32)
    v256 = jnp.concatenate([a_ln_g[:, None], a_ln_b[:, None], b_conv_b[:, None], b_ln_g[:, None],
                            b_ln_b[:, None], d_scale[:, None], c_conv_w, zeros256], axis=1)
    v1024 = jnp.concatenate([norm1_g[:, None], b_gate, jnp.zeros((depth, 2, D_MODEL), F32)], axis=1)
    eye = jnp.eye(len(D_WINDOWS), dtype=F32)
    d_w_bd = jnp.einsum('lgcd,gh->lgchd', d_w, eye).reshape(depth, BR, BR).astype(BF16)
    lw = {
        "v1024": v1024, "v256": v256, "w_in": w_in.astype(BF16), "b_conv_w": b_conv_w,
        "d_w_bd": d_w_bd, "w_branch": w_branch.astype(BF16), "w_gate": w_gate.astype(BF16),
        "w_out": w_out.astype(BF16), "norm2_g": norm2_g[:, None],
        "w_ffn_gate": w_ffn_gate.astype(BF16), "w_ffn_up": w_ffn_up.astype(BF16),
        "w_ffn_down": w_ffn_down.astype(BF16),
    }
    lw_p = dict(lw)
    lw_p["a_w"], lw_p["a_bias"] = _spatial_layout(a_ws, a_bs, min(A_BLOCK, seq))
    lw_s = dict(lw)
    lw_s["a_w"], lw_s["a_bias"] = _spatial_layout(a_ws, a_bs, min(A_BLOCK, dec_seq))
    return lw, lw_p, lw_s


def kernel(x_prompt, x_sample, mem_prompt, cache_mem_k, cache_mem_v, state_conv_b, state_conv_c, state_pool_d, norm1_g, mem_norm_g, w_in, a_ln_g, a_ln_b, a_ws, a_bs, b_conv_w, b_conv_b, b_ln_g, b_ln_b, c_conv_w, d_w, d_scale, w_mem_kv, w_branch, w_gate, b_gate, w_out, norm2_g, w_ffn_gate, w_ffn_up, w_ffn_down, final_norm_g):
    batch, seq, _ = x_prompt.shape
    dec_batch, dec_seq, _ = x_sample.shape
    depth = w_in.shape[0]
    assert depth == DEPTH
    lw, lw_p, lw_s = _layout_params(
        seq, dec_seq, norm1_g, w_in, a_ln_g, a_ln_b, a_ws, a_bs, b_conv_w, b_conv_b, b_ln_g, b_ln_b,
        c_conv_w, d_w, d_scale, w_branch, w_gate, b_gate, w_out, norm2_g, w_ffn_gate, w_ffn_up, w_ffn_down)
    final_g = final_norm_g[None]

    mk_p, mv_p = _memory_kv(mem_prompt.reshape(batch * N_MEM, D_MODEL), mem_norm_g[:, None],
                            w_mem_kv.astype(BF16), row_tile=1024)
    k_p = mk_p.reshape(depth * batch, N_MEM, BR)
    v_p = mv_p.reshape(depth * batch, N_MEM, BR)
    k_s = cache_mem_k.reshape(depth * dec_batch, N_MEM, BR)
    v_s = cache_mem_v.reshape(depth * dec_batch, N_MEM, BR)
    states = _layout_states(state_conv_b, state_conv_c, state_pool_d)

    xp = x_prompt.reshape(batch * seq, D_MODEL)
    xs = x_sample.reshape(dec_batch * dec_seq, D_MODEL)
    cb_p, cc_p, pd_p, av_s, cb_s, cc_s, pd_s = [], [], [], [], [], [], []
    for l in range(depth):
        last = l == depth - 1
        xp, hb, hc, hd = _mixer_prompt_call(xp, k_p, v_p, lw_p, l, batch, seq)
        xp = _ffn_call(xp, lw, l, final_g, last)
        cb_p.append(hb); cc_p.append(hc); pd_p.append(hd)
        xs, hb, hc, hd, va = _mixer_state_call(xs, k_s, v_s, states, lw_s, l, dec_batch, dec_seq)
        xs = _ffn_call(xs, lw, l, final_g, last)
        cb_s.append(hb); cc_s.append(hc); pd_s.append(hd)
        av_s.append(va.reshape(dec_batch, dec_seq, BR))

    def tails(parts, hpad, hist):
        return jnp.stack(parts)[:, :, hpad - hist:, :]

    kv_shape = (depth, batch, N_MEM, X_HEADS, X_HEAD_DIM)
    return (xp.reshape(batch, seq, D_MODEL), xs.reshape(dec_batch, dec_seq, D_MODEL),
            mk_p.reshape(kv_shape), mv_p.reshape(kv_shape),
            tails(cb_p, B_HPAD, B_KERNEL - 1), tails(cc_p, C_HPAD, C_KERNEL - 1), tails(pd_p, D_HPAD, D_HIST),
            jnp.stack(av_s),
            tails(cb_s, B_HPAD, B_KERNEL - 1), tails(cc_s, C_HPAD, C_KERNEL - 1), tails(pd_s, D_HPAD, D_HIST))
```

```python
import functools

import jax
import jax.numpy as jnp
from jax import lax
from jax.experimental import pallas as pl
from jax.experimental.pallas import tpu as pltpu

D_MODEL = 1024
DEPTH = 4
CHUNK = 64
PAST_LEN = 2048
BR = D_MODEL // 4
N_BRANCH = 5
A_BLOCK = 128
A_GROUPS = 4
B_KERNEL = 31
C_KERNEL = 3
D_WINDOWS = (2, 4, 8, 16)
D_HIST = max(D_WINDOWS) - 1
N_MEM = 256
X_HEADS = 4
X_HEAD_DIM = BR // X_HEADS
D_FF = ((8 * D_MODEL // 3 + 255) // 256) * 256
W_IN_COLS = 9 * BR
EPS = 1e-6

SUBLANES = 8
LANES = 128
B_HPAD = 32
C_HPAD = 8
D_HPAD = 16
VMEM_LIMIT = 56 * 1024 * 1024
GATE_COLS = 256
MIXER_SUBTILES = 2
FF_CHUNK = 1024

BF16 = jnp.bfloat16
F32 = jnp.float32

V256_A_LN_G, V256_A_LN_B, V256_B_CB, V256_B_LN_G, V256_B_LN_B, V256_D_SCALE, V256_C_W0 = range(7)
V1024_NORM1, V1024_BGATE0 = 0, 1


def _dot(a, b):
    return jnp.dot(a, b, preferred_element_type=F32)


def _sigmoid(x):
    return 0.5 * jnp.tanh(0.5 * x) + 0.5


def _rms(x, g):
    return x * lax.rsqrt(jnp.mean(x * x, axis=-1, keepdims=True) + EPS) * g


def _layernorm(x, g, b):
    mu = jnp.mean(x, axis=-1, keepdims=True)
    xc = x - mu
    var = jnp.mean(xc * xc, axis=-1, keepdims=True)
    return xc * lax.rsqrt(var + EPS) * g + b


def _lane_group_select(lane_group, vals):
    out = vals[-1]
    for g in range(len(vals) - 2, -1, -1):
        out = jnp.where(lane_group == g, vals[g], out)
    return out


def _shift_up(x, n):
    return x if n == 0 else pltpu.roll(x, x.shape[0] - n, axis=0)


def _shift_down(x, n):
    return x if n == 0 else pltpu.roll(x, n, axis=0)


def _causal_conv31(prev, cur, w_ref, bias):
    ch = cur.shape[0]
    ext = jnp.concatenate([prev, cur], axis=0)
    lead = B_HPAD - (B_KERNEL - 1)
    acc = bias + w_ref(B_KERNEL - 1) * cur
    for b in range(SUBLANES):
        shifted = _shift_up(ext, b)
        for a in range(B_HPAD // SUBLANES):
            k = a * SUBLANES + b - lead
            if 0 <= k < B_KERNEL - 1:
                acc = acc + w_ref(k) * shifted[a * SUBLANES:a * SUBLANES + ch]
    return acc


def _mixer_kernel(*refs, n_seg, seg_len, has_state, pos0, emit_va, row_chunk, n_sub):
    it = iter(refs)
    x_ref, k_ref, v_ref = next(it), next(it), next(it)
    if has_state:
        hb_ref, hc_ref, hd_ref = next(it), next(it), next(it)
    v1024_ref, w_in_ref, v256_ref, aw_ref, abias_ref = (next(it) for _ in range(5))
    bcw_ref, dwbd_ref, wbr_ref, wg_ref, wout_ref = (next(it) for _ in range(5))
    x1_ref, hbo_ref, hco_ref, hdo_ref = (next(it) for _ in range(4))
    vao_ref = next(it) if emit_va else None
    xnb_ref, z_ref, hist_b, hist_c, hist_d, br_ref, ua_ref, va_ref, gate_ref = (next(it) for _ in range(9))

    rows = n_seg * seg_len
    j = pl.program_id(1)

    def v256(r, lanes=slice(None)):
        return v256_ref[0, r:r + 1, lanes]

    @pl.when(j == 0)
    def _():
        if has_state:
            hist_b[...] = hb_ref[...]
            hist_c[...] = hc_ref[...]
            hist_d[...] = hd_ref[...]
        else:
            hist_b[...] = jnp.zeros(hist_b.shape, F32)
            hist_c[...] = jnp.zeros(hist_c.shape, F32)
            hist_d[...] = jnp.zeros(hist_d.shape, F32)

    lane_group = lax.broadcasted_iota(jnp.int32, (1, BR), 1) // (BR // 4)
    d_window = _lane_group_select(lane_group, [jnp.full((1, BR), w, jnp.int32) for w in D_WINDOWS])
    half_lo = lax.broadcasted_iota(jnp.int32, (1, LANES), 1) < LANES // 2

    ch = row_chunk

    def chunk_task(s, c, prev_b, prev_c, prev_d):
        t0 = c * ch
        rs = pl.ds(s * seg_len + t0, ch)

        ga = jax.nn.gelu(z_ref[rs, 0:2 * BR])
        ua_ref[rs, :] = ga[:, :BR]
        va = _layernorm(ga[:, BR:], v256(V256_A_LN_G), v256(V256_A_LN_B))
        va_ref[rs, :] = va
        if emit_va:
            vao_ref[rs, :] = va

        zb = z_ref[rs, 2 * BR:4 * BR]
        b_in = zb[:, :BR] * _sigmoid(zb[:, BR:])
        conv = []
        for h in range(BR // LANES):
            ls = slice(h * LANES, (h + 1) * LANES)
            conv.append(_causal_conv31(prev_b[:, ls], b_in[:, ls],
                                       lambda k, ls=ls: bcw_ref[0, k:k + 1, ls], v256(V256_B_CB, ls)))
        lnb = _layernorm(jnp.concatenate(conv, axis=1), v256(V256_B_LN_G), v256(V256_B_LN_B))
        br_ref[1, rs, :] = (lnb * _sigmoid(lnb)).astype(BF16)
        prev_b = jnp.concatenate([prev_b, b_in], axis=0)[-B_HPAD:]

        zc = z_ref[rs, 4 * BR:7 * BR]
        cc = zc[:, BR:2 * BR] * zc[:, 2 * BR:]
        ext_c = jnp.concatenate([prev_c, cc], axis=0)
        conv_c = v256(V256_C_W0 + C_KERNEL - 1) * cc
        for k in range(C_KERNEL - 1):
            off = C_HPAD - (C_KERNEL - 1) + k
            conv_c = conv_c + v256(V256_C_W0 + k) * _shift_up(ext_c, off)[:ch]
        br_ref[2, rs, :] = (zc[:, :BR] * conv_c).astype(BF16)
        prev_c = ext_c[-C_HPAD:]

        zd = z_ref[rs, 7 * BR:8 * BR]
        ext_d = jnp.concatenate([prev_d, zd], axis=0)
        wins = []
        for h in range(BR // LANES):
            e = ext_d[:, h * LANES:(h + 1) * LANES]
            s2 = e + _shift_down(e, 1)
            s4 = s2 + _shift_down(s2, 2)
            if h == 0:
                lo, hi = s2[D_HPAD:], s4[D_HPAD:]
            else:
                s8 = s4 + _shift_down(s4, 4)
                lo, hi = s8[D_HPAD:], s8[D_HPAD:] + s8[D_HPAD - 8:-8]
            wins.append(jnp.where(half_lo, lo, hi))
        win = jnp.concatenate(wins, axis=1)
        pos1 = pos0 + j * seg_len + t0 + 1 + lax.broadcasted_iota(jnp.int32, (ch, 1), 0)
        cnt = jnp.minimum(d_window, pos1).astype(F32)
        br_ref[3, rs, :] = (win / cnt - zd).astype(BF16)
        return prev_b, prev_c, ext_d[-D_HPAD:]

    def gate_task(n, cb, rs):
        cols = slice(cb * GATE_COLS, (cb + 1) * GATE_COLS)
        pre = _dot(xnb_ref[rs, :], wg_ref[0, n, :, cols])
        gate_ref[n, rs, cols] = _sigmoid(pre + v1024_ref[0, V1024_BGATE0 + n:V1024_BGATE0 + n + 1, cols])

    blk_seg = min(A_BLOCK, seg_len)
    ri = lax.broadcasted_iota(jnp.int32, (A_BLOCK, A_BLOCK), 0)
    ci = lax.broadcasted_iota(jnp.int32, (A_BLOCK, A_BLOCK), 1)
    keep = (ri // blk_seg == ci // blk_seg) & ((ci % blk_seg) // CHUNK <= (ri % blk_seg) // CHUNK)
    wsp = [jnp.where(keep, aw_ref[0, g], 0.0).astype(BF16) for g in range(A_GROUPS)]

    def spatial_task(blk):
        rs = pl.ds(blk * A_BLOCK, A_BLOCK)
        vb = va_ref[rs, :].astype(BF16)
        sp = _lane_group_select(lane_group, [_dot(wsp[g], vb) for g in range(A_GROUPS)])
        br_ref[0, rs, :] = (ua_ref[rs, :] * (sp + abias_ref[0])).astype(BF16)

    attn_acc = {}

    def attn_task(s, h, rs):
        q = z_ref[rs, 8 * BR:9 * BR] * (X_HEAD_DIM ** -0.5)
        qh = jnp.where(lane_group == h, q, 0.0).astype(BF16)
        sc = lax.dot_general(qh, k_ref[s].astype(BF16), (((1,), (1,)), ((), ())), preferred_element_type=F32)
        p = jnp.exp(sc - jnp.max(sc, axis=-1, keepdims=True))
        inv = 1.0 / jnp.sum(p, axis=-1, keepdims=True)
        oh = _dot(p.astype(BF16), v_ref[s].astype(BF16)) * inv
        attn_acc[s] = oh if h == 0 else jnp.where(lane_group == h, oh, attn_acc[s])
        if h == X_HEADS - 1:
            br_ref[4, rs, :] = attn_acc.pop(s).astype(BF16)

    sub_rows = rows // n_sub
    next_blk = 0
    hist = None
    for sub in range(n_sub):
        rsub = pl.ds(sub * sub_rows, sub_rows)

        xnb_ref[rsub, :] = _rms(x_ref[rsub, :], v1024_ref[0, V1024_NORM1:V1024_NORM1 + 1, :]).astype(BF16)
        z_ref[rsub, :] = _dot(xnb_ref[rsub, :], w_in_ref[0])

        if n_seg == 1:
            chunk_ids = [(0, c) for c in range(sub * sub_rows // ch, (sub + 1) * sub_rows // ch)]
            attn_ids = [(0, h, rsub) for h in range(X_HEADS)]
        else:
            chunk_ids = [(s, c) for s in range(n_seg) for c in range(seg_len // ch)]
            attn_ids = [(s, h, pl.ds(s * seg_len, seg_len)) for s in range(n_seg) for h in range(X_HEADS)]
        gate_ids = [(n, cb, rsub) for n in range(N_BRANCH) for cb in range(D_MODEL // GATE_COLS)]
        slots = len(chunk_ids)
        for i, (s, c) in enumerate(chunk_ids):
            if c == 0:
                hist = (hist_b[s], hist_c[s], hist_d[s])
            hist = chunk_task(s, c, *hist)
            if c == seg_len // ch - 1:
                hist_b[s], hist_c[s], hist_d[s] = hist
            for ids in gate_ids[i * len(gate_ids) // slots:(i + 1) * len(gate_ids) // slots]:
                gate_task(*ids)
            while (next_blk + 1) * A_BLOCK <= s * seg_len + (c + 1) * ch:
                spatial_task(next_blk)
                next_blk += 1
            for ids in attn_ids[i * len(attn_ids) // slots:(i + 1) * len(attn_ids) // slots]:
                attn_task(*ids)

        merged = None
        for n in range(N_BRANCH):
            bn = br_ref[n, rsub, :]
            if n == 3:
                bn = (_dot(bn, dwbd_ref[0]) * v256(V256_D_SCALE)).astype(BF16)
            term = gate_ref[n, rsub, :] * _dot(bn, wbr_ref[0, n])
            merged = term if merged is None else merged + term
        x1_ref[rsub, :] = x_ref[rsub, :] + _dot(merged.astype(BF16), wout_ref[0])

    hbo_ref[...] = hist_b[...]
    hco_ref[...] = hist_c[...]
    hdo_ref[...] = hist_d[...]


def _resident(shape, layer):
    nd = len(shape)
    return pl.BlockSpec((1,) + tuple(shape[1:]), lambda *_: (layer,) + (0,) * (nd - 1),
                        pipeline_mode=pl.Buffered(1))


def _mixer(x, k_all, v_all, states, lw, layer, *, n_seq, seq_len, n_seg, seg_len, pos0, emit_va):
    has_state = states is not None
    rows = n_seg * seg_len
    tiles = seq_len // seg_len
    assert seq_len % seg_len == 0 and n_seq % n_seg == 0 and rows % A_BLOCK == 0
    assert tiles == 1 or n_seg == 1
    assert seg_len % A_BLOCK == 0 or A_BLOCK % seg_len == 0
    assert seg_len >= B_HPAD
    row_chunk = min(64, seg_len)
    n_sub = MIXER_SUBTILES if n_seg == 1 and seg_len % (MIXER_SUBTILES * A_BLOCK) == 0 else 1
    grid = (n_seq // n_seg, tiles)
    per_layer = n_seq // n_seg

    in_specs = [
        pl.BlockSpec((rows, D_MODEL), lambda b, j: (b * tiles + j, 0)),
        pl.BlockSpec((n_seg, N_MEM, BR), lambda b, j: (layer * per_layer + b, 0, 0)),
        pl.BlockSpec((n_seg, N_MEM, BR), lambda b, j: (layer * per_layer + b, 0, 0)),
    ]
    args = [x, k_all, v_all]
    if has_state:
        for st, hpad in zip(states, (B_HPAD, C_HPAD, D_HPAD)):
            in_specs.append(pl.BlockSpec((n_seg, hpad, BR), lambda b, j: (layer * per_layer + b, 0, 0)))
            args.append(st)
    names = ("v1024", "w_in", "v256", "a_w", "a_bias", "b_conv_w", "d_w_bd", "w_branch", "w_gate", "w_out")
    for name in names:
        in_specs.append(_resident(lw[name].shape, layer))
        args.append(lw[name])

    out_shape = [
        jax.ShapeDtypeStruct((n_seq * seq_len, D_MODEL), F32),
        jax.ShapeDtypeStruct((n_seq, B_HPAD, BR), F32),
        jax.ShapeDtypeStruct((n_seq, C_HPAD, BR), F32),
        jax.ShapeDtypeStruct((n_seq, D_HPAD, BR), F32),
    ]
    out_specs = [
        pl.BlockSpec((rows, D_MODEL), lambda b, j: (b * tiles + j, 0)),
        pl.BlockSpec((n_seg, B_HPAD, BR), lambda b, j: (b, 0, 0)),
        pl.BlockSpec((n_seg, C_HPAD, BR), lambda b, j: (b, 0, 0)),
        pl.BlockSpec((n_seg, D_HPAD, BR), lambda b, j: (b, 0, 0)),
    ]
    if emit_va:
        out_shape.append(jax.ShapeDtypeStruct((n_seq * seq_len, BR), F32))
        out_specs.append(pl.BlockSpec((rows, BR), lambda b, j: (b * tiles + j, 0)))

    scratch = [
        pltpu.VMEM((rows, D_MODEL), BF16),
        pltpu.VMEM((rows, W_IN_COLS), F32),
        pltpu.VMEM((n_seg, B_HPAD, BR), F32),
        pltpu.VMEM((n_seg, C_HPAD, BR), F32),
        pltpu.VMEM((n_seg, D_HPAD, BR), F32),
        pltpu.VMEM((N_BRANCH, rows, BR), BF16),
        pltpu.VMEM((rows, BR), F32),
        pltpu.VMEM((rows, BR), F32),
        pltpu.VMEM((N_BRANCH, rows, D_MODEL), F32),
    ]
    kern = functools.partial(_mixer_kernel, n_seg=n_seg, seg_len=seg_len, has_state=has_state,
                             pos0=pos0, emit_va=emit_va, row_chunk=row_chunk, n_sub=n_sub)
    return pl.pallas_call(
        kern, grid=grid, in_specs=in_specs, out_specs=out_specs, out_shape=out_shape,
        scratch_shapes=scratch, name="mixer_state" if has_state else "mixer_prompt",
        compiler_params=pltpu.CompilerParams(dimension_semantics=("arbitrary", "arbitrary"),
                                             vmem_limit_bytes=VMEM_LIMIT),
    )(*args)


def _ffn_kernel(x_ref, g_ref, wg_ref, wu_ref, wd_ref, gf_ref, o_ref, *, final_norm):
    x = x_ref[...]
    hn = _rms(x, g_ref[0]).astype(BF16)
    y = x
    for c0 in range(0, D_FF, FF_CHUNK):
        c1 = min(c0 + FF_CHUNK, D_FF)
        hg = _dot(hn, wg_ref[0, :, c0:c1])
        act = (hg * _sigmoid(hg) * _dot(hn, wu_ref[0, :, c0:c1])).astype(BF16)
        y = y + _dot(act, wd_ref[0, c0:c1, :])
    if final_norm:
        y = _rms(y, gf_ref[...])
    o_ref[...] = y


def _ffn(x, lw, layer, final_g, *, row_tile, final_norm):
    n_rows = x.shape[0]
    assert n_rows % row_tile == 0
    names = ("norm2_g", "w_ffn_gate", "w_ffn_up", "w_ffn_down")
    in_specs = [pl.BlockSpec((row_tile, D_MODEL), lambda i: (i, 0))]
    in_specs += [_resident(lw[n].shape, layer) for n in names]
    in_specs.append(pl.BlockSpec((1, D_MODEL), lambda i: (0, 0)))
    return pl.pallas_call(
        functools.partial(_ffn_kernel, final_norm=final_norm),
        grid=(n_rows // row_tile,), in_specs=in_specs,
        out_specs=pl.BlockSpec((row_tile, D_MODEL), lambda i: (i, 0)),
        out_shape=jax.ShapeDtypeStruct((n_rows, D_MODEL), F32), name="ffn",
        compiler_params=pltpu.CompilerParams(dimension_semantics=("arbitrary",),
                                             vmem_limit_bytes=VMEM_LIMIT),
    )(x, *[lw[n] for n in names], final_g)


def _memkv_kernel(m_ref, g_ref, w_ref, k_ref, v_ref):
    kv = _dot(_rms(m_ref[...], g_ref[0]).astype(BF16), w_ref[0])
    k_ref[0] = kv[:, :BR]
    v_ref[0] = kv[:, BR:]


def _memory_kv(mem_rows, mem_norm_g, w_mem_kv, *, row_tile):
    n_rows = mem_rows.shape[0]
    assert n_rows % row_tile == 0
    tiles = n_rows // row_tile
    out = jax.ShapeDtypeStruct((DEPTH, n_rows, BR), F32)
    return pl.pallas_call(
        _memkv_kernel, grid=(DEPTH, tiles),
        in_specs=[pl.BlockSpec((row_tile, D_MODEL), lambda l, i: (i, 0)),
                  pl.BlockSpec((1, 1, D_MODEL), lambda l, i: (l, 0, 0)),
                  pl.BlockSpec((1, D_MODEL, 2 * BR), lambda l, i: (l, 0, 0))],
        out_specs=[pl.BlockSpec((1, row_tile, BR), lambda l, i: (l, i, 0))] * 2,
        out_shape=[out, out], name="memory_kv",
        compiler_params=pltpu.CompilerParams(dimension_semantics=("arbitrary", "arbitrary"),
                                             vmem_limit_bytes=VMEM_LIMIT),
    )(mem_rows, mem_norm_g, w_mem_kv)


MIXER_ROWS = 512
FFN_ROWS = 1024


def _mixer_prompt_call(x, k_all, v_all, lw, layer, batch, seq):
    return _mixer(x, k_all, v_all, None, lw, layer, n_seq=batch, seq_len=seq, n_seg=1,
                  seg_len=min(MIXER_ROWS, seq), pos0=0, emit_va=False)


def _mixer_state_call(x, k_all, v_all, states, lw, layer, dec_batch, dec_seq):
    return _mixer(x, k_all, v_all, states, lw, layer, n_seq=dec_batch, seq_len=dec_seq,
                  n_seg=8, seg_len=dec_seq, pos0=PAST_LEN, emit_va=True)


def _ffn_call(x, lw, layer, final_g, final_norm):
    return _ffn(x, lw, layer, final_g, row_tile=min(FFN_ROWS, x.shape[0]), final_norm=final_norm)


def _pad_rows_front(a, total):
    pad = total - a.shape[-2]
    return jnp.pad(a, [(0, 0)] * (a.ndim - 2) + [(pad, 0), (0, 0)])


def _spatial_layout(a_ws, a_bs, blk_seg):
    reps = A_BLOCK // blk_seg
    w = jnp.tile(a_ws[:, :, :blk_seg, :blk_seg], (1, 1, reps, reps))
    bias = jnp.repeat(jnp.swapaxes(a_bs, 1, 2), BR // A_GROUPS, axis=2)
    bias = jnp.tile(bias[:, :blk_seg, :], (1, reps, 1))
    return w, bias


def _layout_states(state_conv_b, state_conv_c, state_pool_d):
    return tuple(_pad_rows_front(st, hpad).reshape(-1, hpad, BR)
                 for st, hpad in ((state_conv_b, B_HPAD), (state_conv_c, C_HPAD), (state_pool_d, D_HPAD)))


def _layout_params(seq, dec_seq, norm1_g, w_in, a_ln_g, a_ln_b, a_ws, a_bs, b_conv_w, b_conv_b, b_ln_g,
                   b_ln_b, c_conv_w, d_w, d_scale, w_branch, w_gate, b_gate, w_out, norm2_g, w_ffn_gate,
                   w_ffn_up, w_ffn_down):
    depth = w_in.shape[0]
    zeros256 = jnp.zeros((depth, 16 - 6 - C_KERNEL, BR), F32)
    v256 = jnp.concatenate([a_ln_g[:, None], a_ln_b[:, None], b_conv_b[:, None], b_ln_g[:, None],
                            b_ln_b[:, None], d_scale[:, None], c_conv_w, zeros256], axis=1)
    v1024 = jnp.concatenate([norm1_g[:, None], b_gate, jnp.zeros((depth, 2, D_MODEL), F32)], axis=1)
    eye = jnp.eye(len(D_WINDOWS), dtype=F32)
    d_w_bd = jnp.einsum('lgcd,gh->lgchd', d_w, eye).reshape(depth, BR, BR).astype(BF16)
    lw = {
        "v1024": v1024, "v256": v256, "w_in": w_in.astype(BF16), "b_conv_w": b_conv_w,
        "d_w_bd": d_w_bd, "w_branch": w_branch.astype(BF16), "w_gate": w_gate.astype(BF16),
        "w_out": w_out.astype(BF16), "norm2_g": norm2_g[:, None],
        "w_ffn_gate": w_ffn_gate.astype(BF16), "w_ffn_up": w_ffn_up.astype(BF16),
        "w_ffn_down": w_ffn_down.astype(BF16),
    }
    lw_p = dict(lw)
    lw_p["a_w"], lw_p["a_bias"] = _spatial_layout(a_ws, a_bs, min(A_BLOCK, seq))
    lw_s = dict(lw)
    lw_s["a_w"], lw_s["a_bias"] = _spatial_layout(a_ws, a_bs, min(A_BLOCK, dec_seq))
    return lw, lw_p, lw_s


def kernel(x_prompt, x_sample, mem_prompt, cache_mem_k, cache_mem_v, state_conv_b, state_conv_c, state_pool_d, norm1_g, mem_norm_g, w_in, a_ln_g, a_ln_b, a_ws, a_bs, b_conv_w, b_conv_b, b_ln_g, b_ln_b, c_conv_w, d_w, d_scale, w_mem_kv, w_branch, w_gate, b_gate, w_out, norm2_g, w_ffn_gate, w_ffn_up, w_ffn_down, final_norm_g):
    batch, seq, _ = x_prompt.shape
    dec_batch, dec_seq, _ = x_sample.shape
    depth = w_in.shape[0]
    assert depth == DEPTH
    lw, lw_p, lw_s = _layout_params(
        seq, dec_seq, norm1_g, w_in, a_ln_g, a_ln_b, a_ws, a_bs, b_conv_w, b_conv_b, b_ln_g, b_ln_b,
        c_conv_w, d_w, d_scale, w_branch, w_gate, b_gate, w_out, norm2_g, w_ffn_gate, w_ffn_up, w_ffn_down)
    final_g = final_norm_g[None]

    mk_p, mv_p = _memory_kv(mem_prompt.reshape(batch * N_MEM, D_MODEL), mem_norm_g[:, None],
                            w_mem_kv.astype(BF16), row_tile=1024)
    k_p = mk_p.reshape(depth * batch, N_MEM, BR)
    v_p = mv_p.reshape(depth * batch, N_MEM, BR)
    k_s = cache_mem_k.reshape(depth * dec_batch, N_MEM, BR)
    v_s = cache_mem_v.reshape(depth * dec_batch, N_MEM, BR)
    states = _layout_states(state_conv_b, state_conv_c, state_pool_d)

    xp = x_prompt.reshape(batch * seq, D_MODEL)
    xs = x_sample.reshape(dec_batch * dec_seq, D_MODEL)
    cb_p, cc_p, pd_p, av_s, cb_s, cc_s, pd_s = [], [], [], [], [], [], []
    for l in range(depth):
        last = l == depth - 1
        xp, hb, hc, hd = _mixer_prompt_call(xp, k_p, v_p, lw_p, l, batch, seq)
        xp = _ffn_call(xp, lw, l, final_g, last)
        cb_p.append(hb); cc_p.append(hc); pd_p.append(hd)
        xs, hb, hc, hd, va = _mixer_state_call(xs, k_s, v_s, states, lw_s, l, dec_batch, dec_seq)
        xs = _ffn_call(xs, lw, l, final_g, last)
        cb_s.append(hb); cc_s.append(hc); pd_s.append(hd)
        av_s.append(va.reshape(dec_batch, dec_seq, BR))

    def tails(parts, hpad, hist):
        return jnp.stack(parts)[:, :, hpad - hist:, :]

    kv_shape = (depth, batch, N_MEM, X_HEADS, X_HEAD_DIM)
    return (xp.reshape(batch, seq, D_MODEL), xs.reshape(dec_batch, dec_seq, D_MODEL),
            mk_p.reshape(kv_shape), mv_p.reshape(kv_shape),
            tails(cb_p, B_HPAD, B_KERNEL - 1), tails(cc_p, C_HPAD, C_KERNEL - 1), tails(pd_p, D_HPAD, D_HIST),
            jnp.stack(av_s),
            tails(cb_s, B_HPAD, B_KERNEL - 1), tails(cc_s, C_HPAD, C_KERNEL - 1), tails(pd_s, D_HPAD, D_HIST))
```

```python
import functools

import jax
import jax.numpy as jnp
from jax import lax
from jax.experimental import pallas as pl
from jax.experimental.pallas import tpu as pltpu

D_MODEL = 1024
DEPTH = 4
CHUNK = 64
PAST_LEN = 2048
BR = D_MODEL // 4
N_BRANCH = 5
A_BLOCK = 128
A_GROUPS = 4
B_KERNEL = 31
C_KERNEL = 3
D_WINDOWS = (2, 4, 8, 16)
D_HIST = max(D_WINDOWS) - 1
N_MEM = 256
X_HEADS = 4
X_HEAD_DIM = BR // X_HEADS
D_FF = ((8 * D_MODEL // 3 + 255) // 256) * 256
W_IN_COLS = 9 * BR
EPS = 1e-6

SUBLANES = 8
LANES = 128
B_HPAD = 32
C_HPAD = 8
D_HPAD = 16
VMEM_LIMIT = 56 * 1024 * 1024
GATE_COLS = 256
MIXER_SUBTILES = 2
FF_CHUNK = 1024
ATTN_ROWS = 256

BF16 = jnp.bfloat16
F32 = jnp.float32

V256_A_LN_G, V256_A_LN_B, V256_B_CB, V256_B_LN_G, V256_B_LN_B, V256_D_SCALE, V256_C_W0 = range(7)
V1024_NORM1, V1024_BGATE0 = 0, 1


def _dot(a, b):
    return jnp.dot(a, b, preferred_element_type=F32)


def _sigmoid(x):
    return 0.5 * jnp.tanh(0.5 * x) + 0.5


def _rms(x, g):
    return x * lax.rsqrt(jnp.mean(x * x, axis=-1, keepdims=True) + EPS) * g


def _layernorm(x, g, b):
    mu = jnp.mean(x, axis=-1, keepdims=True)
    xc = x - mu
    var = jnp.mean(xc * xc, axis=-1, keepdims=True)
    return xc * lax.rsqrt(var + EPS) * g + b


def _lane_group_select(lane_group, vals):
    out = vals[-1]
    for g in range(len(vals) - 2, -1, -1):
        out = jnp.where(lane_group == g, vals[g], out)
    return out


def _shift_up(x, n):
    return x if n == 0 else pltpu.roll(x, x.shape[0] - n, axis=0)


def _shift_down(x, n):
    return x if n == 0 else pltpu.roll(x, n, axis=0)


def _causal_conv31(prev, cur, w_ref, bias):
    ch = cur.shape[0]
    ext = jnp.concatenate([prev, cur], axis=0)
    lead = B_HPAD - (B_KERNEL - 1)
    acc = bias + w_ref(B_KERNEL - 1) * cur
    for b in range(SUBLANES):
        shifted = _shift_up(ext, b)
        for a in range(B_HPAD // SUBLANES):
            k = a * SUBLANES + b - lead
            if 0 <= k < B_KERNEL - 1:
                acc = acc + w_ref(k) * shifted[a * SUBLANES:a * SUBLANES + ch]
    return acc


def _mixer_kernel(*refs, n_seg, seg_len, has_state, pos0, emit_va, row_chunk, n_sub):
    it = iter(refs)
    x_ref, k_ref, v_ref = next(it), next(it), next(it)
    if has_state:
        hb_ref, hc_ref, hd_ref = next(it), next(it), next(it)
    v1024_ref, w_in_ref, v256_ref, aw_ref, abias_ref = (next(it) for _ in range(5))
    bcw_ref, dwbd_ref, wbr_ref, wg_ref, wout_ref = (next(it) for _ in range(5))
    x1_ref, hbo_ref, hco_ref, hdo_ref = (next(it) for _ in range(4))
    vao_ref = next(it) if emit_va else None
    xnb_ref, z_ref, hist_b, hist_c, hist_d, br_ref, ua_ref, va_ref, gate_ref = (next(it) for _ in range(9))

    rows = n_seg * seg_len
    j = pl.program_id(1)

    def v256(r, lanes=slice(None)):
        return v256_ref[0, r:r + 1, lanes]

    @pl.when(j == 0)
    def _():
        if has_state:
            hist_b[...] = hb_ref[...]
            hist_c[...] = hc_ref[...]
            hist_d[...] = hd_ref[...]
        else:
            hist_b[...] = jnp.zeros(hist_b.shape, F32)
            hist_c[...] = jnp.zeros(hist_c.shape, F32)
            hist_d[...] = jnp.zeros(hist_d.shape, F32)

    lane_group = lax.broadcasted_iota(jnp.int32, (1, BR), 1) // (BR // 4)
    d_window = _lane_group_select(lane_group, [jnp.full((1, BR), w, jnp.int32) for w in D_WINDOWS])
    half_lo = lax.broadcasted_iota(jnp.int32, (1, LANES), 1) < LANES // 2

    ch = row_chunk

    def chunk_task(s, c, prev_b, prev_c, prev_d):
        t0 = c * ch
        rs = pl.ds(s * seg_len + t0, ch)

        ga = jax.nn.gelu(z_ref[rs, 0:2 * BR])
        ua_ref[rs, :] = ga[:, :BR]
        va = _layernorm(ga[:, BR:], v256(V256_A_LN_G), v256(V256_A_LN_B))
        va_ref[rs, :] = va
        if emit_va:
            vao_ref[rs, :] = va

        zb = z_ref[rs, 2 * BR:4 * BR]
        b_in = zb[:, :BR] * _sigmoid(zb[:, BR:])
        conv = []
        for h in range(BR // LANES):
            ls = slice(h * LANES, (h + 1) * LANES)
            conv.append(_causal_conv31(prev_b[:, ls], b_in[:, ls],
                                       lambda k, ls=ls: bcw_ref[0, k:k + 1, ls], v256(V256_B_CB, ls)))
        lnb = _layernorm(jnp.concatenate(conv, axis=1), v256(V256_B_LN_G), v256(V256_B_LN_B))
        br_ref[1, rs, :] = (lnb * _sigmoid(lnb)).astype(BF16)
        prev_b = jnp.concatenate([prev_b, b_in], axis=0)[-B_HPAD:]

        zc = z_ref[rs, 4 * BR:7 * BR]
        cc = zc[:, BR:2 * BR] * zc[:, 2 * BR:]
        ext_c = jnp.concatenate([prev_c, cc], axis=0)
        conv_c = v256(V256_C_W0 + C_KERNEL - 1) * cc
        for k in range(C_KERNEL - 1):
            off = C_HPAD - (C_KERNEL - 1) + k
            conv_c = conv_c + v256(V256_C_W0 + k) * _shift_up(ext_c, off)[:ch]
        br_ref[2, rs, :] = (zc[:, :BR] * conv_c).astype(BF16)
        prev_c = ext_c[-C_HPAD:]

        zd = z_ref[rs, 7 * BR:8 * BR]
        ext_d = jnp.concatenate([prev_d, zd], axis=0)
        wins = []
        for h in range(BR // LANES):
            e = ext_d[:, h * LANES:(h + 1) * LANES]
            s2 = e + _shift_down(e, 1)
            s4 = s2 + _shift_down(s2, 2)
            if h == 0:
                lo, hi = s2[D_HPAD:], s4[D_HPAD:]
            else:
                s8 = s4 + _shift_down(s4, 4)
                lo, hi = s8[D_HPAD:], s8[D_HPAD:] + s8[D_HPAD - 8:-8]
            wins.append(jnp.where(half_lo, lo, hi))
        win = jnp.concatenate(wins, axis=1)
        pos1 = pos0 + j * seg_len + t0 + 1 + lax.broadcasted_iota(jnp.int32, (ch, 1), 0)
        cnt = jnp.minimum(d_window, pos1).astype(F32)
        br_ref[3, rs, :] = (win / cnt - zd).astype(BF16)
        return prev_b, prev_c, ext_d[-D_HPAD:]

    def gate_task(n, cb, rs):
        cols = slice(cb * GATE_COLS, (cb + 1) * GATE_COLS)
        pre = _dot(xnb_ref[rs, :], wg_ref[0, n, :, cols])
        gate_ref[n, rs, cols] = _sigmoid(pre + v1024_ref[0, V1024_BGATE0 + n:V1024_BGATE0 + n + 1, cols])

    blk_seg = min(A_BLOCK, seg_len)
    ri = lax.broadcasted_iota(jnp.int32, (A_BLOCK, A_BLOCK), 0)
    ci = lax.broadcasted_iota(jnp.int32, (A_BLOCK, A_BLOCK), 1)
    keep = (ri // blk_seg == ci // blk_seg) & ((ci % blk_seg) // CHUNK <= (ri % blk_seg) // CHUNK)
    wsp = jnp.concatenate([jnp.where(keep, aw_ref[0, g], 0.0).astype(BF16) for g in range(A_GROUPS)], axis=0)

    def spatial_task(blk):
        rs = pl.ds(blk * A_BLOCK, A_BLOCK)
        sp4 = _dot(wsp, va_ref[rs, :].astype(BF16))
        sp = _lane_group_select(lane_group, [sp4[g * A_BLOCK:(g + 1) * A_BLOCK] for g in range(A_GROUPS)])
        br_ref[0, rs, :] = (ua_ref[rs, :] * (sp + abias_ref[0])).astype(BF16)

    def attn_task(s, r0, nr):
        rs = pl.ds(r0, nr)
        q = z_ref[rs, 8 * BR:9 * BR] * (X_HEAD_DIM ** -0.5)
        q4 = jnp.concatenate([jnp.where(lane_group == h, q, 0.0) for h in range(X_HEADS)], axis=0)
        sc = lax.dot_general(q4.astype(BF16), k_ref[s].astype(BF16), (((1,), (1,)), ((), ())),
                             preferred_element_type=F32)
        p = jnp.exp(sc - jnp.max(sc, axis=-1, keepdims=True))
        inv = 1.0 / jnp.sum(p, axis=-1, keepdims=True)
        o4 = _dot(p.astype(BF16), v_ref[s].astype(BF16)) * inv
        o = _lane_group_select(lane_group, [o4[h * nr:(h + 1) * nr] for h in range(X_HEADS)])
        br_ref[4, rs, :] = o.astype(BF16)

    sub_rows = rows // n_sub
    next_blk = 0
    hist = None
    for sub in range(n_sub):
        rsub = pl.ds(sub * sub_rows, sub_rows)

        xnb_ref[rsub, :] = _rms(x_ref[rsub, :], v1024_ref[0, V1024_NORM1:V1024_NORM1 + 1, :]).astype(BF16)
        z_ref[rsub, :] = _dot(xnb_ref[rsub, :], w_in_ref[0])

        if n_seg == 1:
            chunk_ids = [(0, c) for c in range(sub * sub_rows // ch, (sub + 1) * sub_rows // ch)]
            ar = min(ATTN_ROWS, sub_rows)
            attn_ids = [(0, r0, ar) for r0 in range(sub * sub_rows, (sub + 1) * sub_rows, ar)]
        else:
            chunk_ids = [(s, c) for s in range(n_seg) for c in range(seg_len // ch)]
            attn_ids = [(s, s * seg_len, seg_len) for s in range(n_seg)]
        gate_ids = [(n, cb, rsub) for n in range(N_BRANCH) for cb in range(D_MODEL // GATE_COLS)]
        slots = len(chunk_ids)
        for i, (s, c) in enumerate(chunk_ids):
            if c == 0:
                hist = (hist_b[s], hist_c[s], hist_d[s])
            hist = chunk_task(s, c, *hist)
            if c == seg_len // ch - 1:
                hist_b[s], hist_c[s], hist_d[s] = hist
            for ids in gate_ids[i * len(gate_ids) // slots:(i + 1) * len(gate_ids) // slots]:
                gate_task(*ids)
            while (next_blk + 1) * A_BLOCK <= s * seg_len + (c + 1) * ch:
                spatial_task(next_blk)
                next_blk += 1
            for ids in attn_ids[i * len(attn_ids) // slots:(i + 1) * len(attn_ids) // slots]:
                attn_task(*ids)

        merged = None
        for n in range(N_BRANCH):
            bn = br_ref[n, rsub, :]
            if n == 3:
                bn = (_dot(bn, dwbd_ref[0]) * v256(V256_D_SCALE)).astype(BF16)
            term = gate_ref[n, rsub, :] * _dot(bn, wbr_ref[0, n])
            merged = term if merged is None else merged + term
        x1_ref[rsub, :] = x_ref[rsub, :] + _dot(merged.astype(BF16), wout_ref[0])

    hbo_ref[...] = hist_b[...]
    hco_ref[...] = hist_c[...]
    hdo_ref[...] = hist_d[...]


def _resident(shape, layer):
    nd = len(shape)
    return pl.BlockSpec((1,) + tuple(shape[1:]), lambda *_: (layer,) + (0,) * (nd - 1),
                        pipeline_mode=pl.Buffered(1))


def _mixer(x, k_all, v_all, states, lw, layer, *, n_seq, seq_len, n_seg, seg_len, pos0, emit_va):
    has_state = states is not None
    rows = n_seg * seg_len
    tiles = seq_len // seg_len
    assert seq_len % seg_len == 0 and n_seq % n_seg == 0 and rows % A_BLOCK == 0
    assert tiles == 1 or n_seg == 1
    assert seg_len % A_BLOCK == 0 or A_BLOCK % seg_len == 0
    assert seg_len >= B_HPAD
    row_chunk = min(64, seg_len)
    n_sub = MIXER_SUBTILES if n_seg == 1 and seg_len % (MIXER_SUBTILES * A_BLOCK) == 0 else 1
    grid = (n_seq // n_seg, tiles)
    per_layer = n_seq // n_seg

    in_specs = [
        pl.BlockSpec((rows, D_MODEL), lambda b, j: (b * tiles + j, 0)),
        pl.BlockSpec((n_seg, N_MEM, BR), lambda b, j: (layer * per_layer + b, 0, 0)),
        pl.BlockSpec((n_seg, N_MEM, BR), lambda b, j: (layer * per_layer + b, 0, 0)),
    ]
    args = [x, k_all, v_all]
    if has_state:
        for st, hpad in zip(states, (B_HPAD, C_HPAD, D_HPAD)):
            in_specs.append(pl.BlockSpec((n_seg, hpad, BR), lambda b, j: (layer * per_layer + b, 0, 0)))
            args.append(st)
    names = ("v1024", "w_in", "v256", "a_w", "a_bias", "b_conv_w", "d_w_bd", "w_branch", "w_gate", "w_out")
    for name in names:
        in_specs.append(_resident(lw[name].shape, layer))
        args.append(lw[name])

    out_shape = [
        jax.ShapeDtypeStruct((n_seq * seq_len, D_MODEL), F32),
        jax.ShapeDtypeStruct((n_seq, B_HPAD, BR), F32),
        jax.ShapeDtypeStruct((n_seq, C_HPAD, BR), F32),
        jax.ShapeDtypeStruct((n_seq, D_HPAD, BR), F32),
    ]
    out_specs = [
        pl.BlockSpec((rows, D_MODEL), lambda b, j: (b * tiles + j, 0)),
        pl.BlockSpec((n_seg, B_HPAD, BR), lambda b, j: (b, 0, 0)),
        pl.BlockSpec((n_seg, C_HPAD, BR), lambda b, j: (b, 0, 0)),
        pl.BlockSpec((n_seg, D_HPAD, BR), lambda b, j: (b, 0, 0)),
    ]
    if emit_va:
        out_shape.append(jax.ShapeDtypeStruct((n_seq * seq_len, BR), F32))
        out_specs.append(pl.BlockSpec((rows, BR), lambda b, j: (b * tiles + j, 0)))

    scratch = [
        pltpu.VMEM((rows, D_MODEL), BF16),
        pltpu.VMEM((rows, W_IN_COLS), F32),
        pltpu.VMEM((n_seg, B_HPAD, BR), F32),
        pltpu.VMEM((n_seg, C_HPAD, BR), F32),
        pltpu.VMEM((n_seg, D_HPAD, BR), F32),
        pltpu.VMEM((N_BRANCH, rows, BR), BF16),
        pltpu.VMEM((rows, BR), F32),
        pltpu.VMEM((rows, BR), F32),
        pltpu.VMEM((N_BRANCH, rows, D_MODEL), F32),
    ]
    kern = functools.partial(_mixer_kernel, n_seg=n_seg, seg_len=seg_len, has_state=has_state,
                             pos0=pos0, emit_va=emit_va, row_chunk=row_chunk, n_sub=n_sub)
    return pl.pallas_call(
        kern, grid=grid, in_specs=in_specs, out_specs=out_specs, out_shape=out_shape,
        scratch_shapes=scratch, name="mixer_state" if has_state else "mixer_prompt",
        compiler_params=pltpu.CompilerParams(dimension_semantics=("arbitrary", "arbitrary"),
                                             vmem_limit_bytes=VMEM_LIMIT),
    )(*args)


def _ffn_kernel(x_ref, g_ref, wg_ref, wu_ref, wd_ref, gf_ref, o_ref, *, final_norm):
    x = x_ref[...]
    hn = _rms(x, g_ref[0]).astype(BF16)
    y = x
    for c0 in range(0, D_FF, FF_CHUNK):
        c1 = min(c0 + FF_CHUNK, D_FF)
        hg = _dot(hn, wg_ref[0, :, c0:c1])
        act = (hg * _sigmoid(hg) * _dot(hn, wu_ref[0, :, c0:c1])).astype(BF16)
        y = y + _dot(act, wd_ref[0, c0:c1, :])
    if final_norm:
        y = _rms(y, gf_ref[...])
    o_ref[...] = y


def _ffn(x, lw, layer, final_g, *, row_tile, final_norm):
    n_rows = x.shape[0]
    assert n_rows % row_tile == 0
    names = ("norm2_g", "w_ffn_gate", "w_ffn_up", "w_ffn_down")
    in_specs = [pl.BlockSpec((row_tile, D_MODEL), lambda i: (i, 0))]
    in_specs += [_resident(lw[n].shape, layer) for n in names]
    in_specs.append(pl.BlockSpec((1, D_MODEL), lambda i: (0, 0)))
    return pl.pallas_call(
        functools.partial(_ffn_kernel, final_norm=final_norm),
        grid=(n_rows // row_tile,), in_specs=in_specs,
        out_specs=pl.BlockSpec((row_tile, D_MODEL), lambda i: (i, 0)),
        out_shape=jax.ShapeDtypeStruct((n_rows, D_MODEL), F32), name="ffn",
        compiler_params=pltpu.CompilerParams(dimension_semantics=("arbitrary",),
                                             vmem_limit_bytes=VMEM_LIMIT),
    )(x, *[lw[n] for n in names], final_g)


def _memkv_kernel(m_ref, g_ref, w_ref, k_ref, v_ref):
    kv = _dot(_rms(m_ref[...], g_ref[0]).astype(BF16), w_ref[0])
    k_ref[0] = kv[:, :BR]
    v_ref[0] = kv[:, BR:]


def _memory_kv(mem_rows, mem_norm_g, w_mem_kv, *, row_tile):
    n_rows = mem_rows.shape[0]
    assert n_rows % row_tile == 0
    tiles = n_rows // row_tile
    out = jax.ShapeDtypeStruct((DEPTH, n_rows, BR), F32)
    return pl.pallas_call(
        _memkv_kernel, grid=(DEPTH, tiles),
        in_specs=[pl.BlockSpec((row_tile, D_MODEL), lambda l, i: (i, 0)),
                  pl.BlockSpec((1, 1, D_MODEL), lambda l, i: (l, 0, 0)),
                  pl.BlockSpec((1, D_MODEL, 2 * BR), lambda l, i: (l, 0, 0))],
        out_specs=[pl.BlockSpec((1, row_tile, BR), lambda l, i: (l, i, 0))] * 2,
        out_shape=[out, out], name="memory_kv",
        compiler_params=pltpu.CompilerParams(dimension_semantics=("arbitrary", "arbitrary"),
                                             vmem_limit_bytes=VMEM_LIMIT),
    )(mem_rows, mem_norm_g, w_mem_kv)


MIXER_ROWS = 512
FFN_ROWS = 1024


def _mixer_prompt_call(x, k_all, v_all, lw, layer, batch, seq):
    return _mixer(x, k_all, v_all, None, lw, layer, n_seq=batch, seq_len=seq, n_seg=1,
                  seg_len=min(MIXER_ROWS, seq), pos0=0, emit_va=False)


def _mixer_state_call(x, k_all, v_all, states, lw, layer, dec_batch, dec_seq):
    return _mixer(x, k_all, v_all, states, lw, layer, n_seq=dec_batch, seq_len=dec_seq,
                  n_seg=8, seg_len=dec_seq, pos0=PAST_LEN, emit_va=True)


def _ffn_call(x, lw, layer, final_g, final_norm):
    return _ffn(x, lw, layer, final_g, row_tile=min(FFN_ROWS, x.shape[0]), final_norm=final_norm)


def _pad_rows_front(a, total):
    pad = total - a.shape[-2]
    return jnp.pad(a, [(0, 0)] * (a.ndim - 2) + [(pad, 0), (0, 0)])


def _spatial_layout(a_ws, a_bs, blk_seg):
    reps = A_BLOCK // blk_seg
    w = jnp.tile(a_ws[:, :, :blk_seg, :blk_seg], (1, 1, reps, reps))
    bias = jnp.repeat(jnp.swapaxes(a_bs, 1, 2), BR // A_GROUPS, axis=2)
    bias = jnp.tile(bias[:, :blk_seg, :], (1, reps, 1))
    return w, bias


def _layout_states(state_conv_b, state_conv_c, state_pool_d):
    return tuple(_pad_rows_front(st, hpad).reshape(-1, hpad, BR)
                 for st, hpad in ((state_conv_b, B_HPAD), (state_conv_c, C_HPAD), (state_pool_d, D_HPAD)))


def _layout_params(seq, dec_seq, norm1_g, w_in, a_ln_g, a_ln_b, a_ws, a_bs, b_conv_w, b_conv_b, b_ln_g,
                   b_ln_b, c_conv_w, d_w, d_scale, w_branch, w_gate, b_gate, w_out, norm2_g, w_ffn_gate,
                   w_ffn_up, w_ffn_down):
    depth = w_in.shape[0]
    zeros256 = jnp.zeros((depth, 16 - 6 - C_KERNEL, BR), F32)
    v256 = jnp.concatenate([a_ln_g[:, None], a_ln_b[:, None], b_conv_b[:, None], b_ln_g[:, None],
                            b_ln_b[:, None], d_scale[:, None], c_conv_w, zeros256], axis=1)
    v1024 = jnp.concatenate([norm1_g[:, None], b_gate, jnp.zeros((depth, 2, D_MODEL), F32)], axis=1)
    eye = jnp.eye(len(D_WINDOWS), dtype=F32)
    d_w_bd = jnp.einsum('lgcd,gh->lgchd', d_w, eye).reshape(depth, BR, BR).astype(BF16)
    lw = {
        "v1024": v1024, "v256": v256, "w_in": w_in.astype(BF16), "b_conv_w": b_conv_w,
        "d_w_bd": d_w_bd, "w_branch": w_branch.astype(BF16), "w_gate": w_gate.astype(BF16),
        "w_out": w_out.astype(BF16), "norm2_g": norm2_g[:, None],
        "w_ffn_gate": w_ffn_gate.astype(BF16), "w_ffn_up": w_ffn_up.astype(BF16),
        "w_ffn_down": w_ffn_down.astype(BF16),
    }
    lw_p = dict(lw)
    lw_p["a_w"], lw_p["a_bias"] = _spatial_layout(a_ws, a_bs, min(A_BLOCK, seq))
    lw_s = dict(lw)
    lw_s["a_w"], lw_s["a_bias"] = _spatial_layout(a_ws, a_bs, min(A_BLOCK, dec_seq))
    return lw, lw_p, lw_s


def kernel(x_prompt, x_sample, mem_prompt, cache_mem_k, cache_mem_v, state_conv_b, state_conv_c, state_pool_d, norm1_g, mem_norm_g, w_in, a_ln_g, a_ln_b, a_ws, a_bs, b_conv_w, b_conv_b, b_ln_g, b_ln_b, c_conv_w, d_w, d_scale, w_mem_kv, w_branch, w_gate, b_gate, w_out, norm2_g, w_ffn_gate, w_ffn_up, w_ffn_down, final_norm_g):
    batch, seq, _ = x_prompt.shape
    dec_batch, dec_seq, _ = x_sample.shape
    depth = w_in.shape[0]
    assert depth == DEPTH
    lw, lw_p, lw_s = _layout_params(
        seq, dec_seq, norm1_g, w_in, a_ln_g, a_ln_b, a_ws, a_bs, b_conv_w, b_conv_b, b_ln_g, b_ln_b,
        c_conv_w, d_w, d_scale, w_branch, w_gate, b_gate, w_out, norm2_g, w_ffn_gate, w_ffn_up, w_ffn_down)
    final_g = final_norm_g[None]

    mk_p, mv_p = _memory_kv(mem_prompt.reshape(batch * N_MEM, D_MODEL), mem_norm_g[:, None],
                            w_mem_kv.astype(BF16), row_tile=1024)
    k_p = mk_p.reshape(depth * batch, N_MEM, BR)
    v_p = mv_p.reshape(depth * batch, N_MEM, BR)
    k_s = cache_mem_k.reshape(depth * dec_batch, N_MEM, BR)
    v_s = cache_mem_v.reshape(depth * dec_batch, N_MEM, BR)
    states = _layout_states(state_conv_b, state_conv_c, state_pool_d)

    xp = x_prompt.reshape(batch * seq, D_MODEL)
    xs = x_sample.reshape(dec_batch * dec_seq, D_MODEL)
    cb_p, cc_p, pd_p, av_s, cb_s, cc_s, pd_s = [], [], [], [], [], [], []
    for l in range(depth):
        last = l == depth - 1
        xp, hb, hc, hd = _mixer_prompt_call(xp, k_p, v_p, lw_p, l, batch, seq)
        xp = _ffn_call(xp, lw, l, final_g, last)
        cb_p.append(hb); cc_p.append(hc); pd_p.append(hd)
        xs, hb, hc, hd, va = _mixer_state_call(xs, k_s, v_s, states, lw_s, l, dec_batch, dec_seq)
        xs = _ffn_call(xs, lw, l, final_g, last)
        cb_s.append(hb); cc_s.append(hc); pd_s.append(hd)
        av_s.append(va.reshape(dec_batch, dec_seq, BR))

    def tails(parts, hpad, hist):
        return jnp.stack(parts)[:, :, hpad - hist:, :]

    kv_shape = (depth, batch, N_MEM, X_HEADS, X_HEAD_DIM)
    return (xp.reshape(batch, seq, D_MODEL), xs.reshape(dec_batch, dec_seq, D_MODEL),
            mk_p.reshape(kv_shape), mv_p.reshape(kv_shape),
            tails(cb_p, B_HPAD, B_KERNEL - 1), tails(cc_p, C_HPAD, C_KERNEL - 1), tails(pd_p, D_HPAD, D_HIST),
            jnp.stack(av_s),
            tails(cb_s, B_HPAD, B_KERNEL - 1), tails(cc_s, C_HPAD, C_KERNEL - 1), tails(pd_s, D_HPAD, D_HIST))
```

```python
import functools

import jax
import jax.numpy as jnp
from jax import lax
from jax.experimental import pallas as pl
from jax.experimental.pallas import tpu as pltpu

D_MODEL = 1024
DEPTH = 4
CHUNK = 64
PAST_LEN = 2048
BR = D_MODEL // 4
N_BRANCH = 5
A_BLOCK = 128
A_GROUPS = 4
B_KERNEL = 31
C_KERNEL = 3
D_WINDOWS = (2, 4, 8, 16)
D_HIST = max(D_WINDOWS) - 1
N_MEM = 256
X_HEADS = 4
X_HEAD_DIM = BR // X_HEADS
D_FF = ((8 * D_MODEL // 3 + 255) // 256) * 256
W_IN_COLS = 9 * BR
EPS = 1e-6

SUBLANES = 8
LANES = 128
B_HPAD = 32
C_HPAD = 8
D_HPAD = 16
VMEM_LIMIT = 56 * 1024 * 1024
GATE_COLS = 256
MIXER_SUBTILES = 1
FF_CHUNK = 1024
ATTN_ROWS = 512

BF16 = jnp.bfloat16
F32 = jnp.float32

V256_A_LN_G, V256_A_LN_B, V256_B_CB, V256_B_LN_G, V256_B_LN_B, V256_D_SCALE, V256_C_W0 = range(7)
V1024_NORM1, V1024_BGATE0 = 0, 1


def _dot(a, b):
    return jnp.dot(a, b, preferred_element_type=F32)


def _sigmoid(x):
    return 0.5 * jnp.tanh(0.5 * x) + 0.5


def _rms(x, g):
    return x * lax.rsqrt(jnp.mean(x * x, axis=-1, keepdims=True) + EPS) * g


def _layernorm(x, g, b):
    mu = jnp.mean(x, axis=-1, keepdims=True)
    xc = x - mu
    var = jnp.mean(xc * xc, axis=-1, keepdims=True)
    return xc * lax.rsqrt(var + EPS) * g + b


def _lane_group_select(lane_group, vals):
    out = vals[-1]
    for g in range(len(vals) - 2, -1, -1):
        out = jnp.where(lane_group == g, vals[g], out)
    return out


def _shift_up(x, n):
    return x if n == 0 else pltpu.roll(x, x.shape[0] - n, axis=0)


def _shift_down(x, n):
    return x if n == 0 else pltpu.roll(x, n, axis=0)


def _causal_conv31(prev, cur, w_ref, bias):
    ch = cur.shape[0]
    ext = jnp.concatenate([prev, cur], axis=0)
    lead = B_HPAD - (B_KERNEL - 1)
    acc = bias + w_ref(B_KERNEL - 1) * cur
    for b in range(SUBLANES):
        shifted = _shift_up(ext, b)
        for a in range(B_HPAD // SUBLANES):
            k = a * SUBLANES + b - lead
            if 0 <= k < B_KERNEL - 1:
                acc = acc + w_ref(k) * shifted[a * SUBLANES:a * SUBLANES + ch]
    return acc


def _mixer_kernel(*refs, n_seg, seg_len, has_state, pos0, emit_va, row_chunk, n_sub):
    it = iter(refs)
    x_ref, k_ref, v_ref = next(it), next(it), next(it)
    if has_state:
        hb_ref, hc_ref, hd_ref = next(it), next(it), next(it)
    v1024_ref, w_in_ref, v256_ref, aw_ref, abias_ref = (next(it) for _ in range(5))
    bcw_ref, dwbd_ref, wbr_ref, wg_ref, wout_ref = (next(it) for _ in range(5))
    x1_ref, hbo_ref, hco_ref, hdo_ref = (next(it) for _ in range(4))
    vao_ref = next(it) if emit_va else None
    xnb_ref, z_ref, hist_b, hist_c, hist_d, br_ref, ua_ref, va_ref, gate_ref = (next(it) for _ in range(9))

    rows = n_seg * seg_len
    j = pl.program_id(1)

    def v256(r, lanes=slice(None)):
        return v256_ref[0, r:r + 1, lanes]

    @pl.when(j == 0)
    def _():
        if has_state:
            hist_b[...] = hb_ref[...]
            hist_c[...] = hc_ref[...]
            hist_d[...] = hd_ref[...]
        else:
            hist_b[...] = jnp.zeros(hist_b.shape, F32)
            hist_c[...] = jnp.zeros(hist_c.shape, F32)
            hist_d[...] = jnp.zeros(hist_d.shape, F32)

    lane_group = lax.broadcasted_iota(jnp.int32, (1, BR), 1) // (BR // 4)
    d_window = _lane_group_select(lane_group, [jnp.full((1, BR), w, jnp.int32) for w in D_WINDOWS])
    half_lo = lax.broadcasted_iota(jnp.int32, (1, LANES), 1) < LANES // 2

    ch = row_chunk

    def chunk_task(s, c, prev_b, prev_c, prev_d):
        t0 = c * ch
        rs = pl.ds(s * seg_len + t0, ch)

        ga = jax.nn.gelu(z_ref[rs, 0:2 * BR])
        ua_ref[rs, :] = ga[:, :BR]
        va = _layernorm(ga[:, BR:], v256(V256_A_LN_G), v256(V256_A_LN_B))
        va_ref[rs, :] = va
        if emit_va:
            vao_ref[rs, :] = va

        zb = z_ref[rs, 2 * BR:4 * BR]
        b_in = zb[:, :BR] * _sigmoid(zb[:, BR:])
        conv = []
        for h in range(BR // LANES):
            ls = slice(h * LANES, (h + 1) * LANES)
            conv.append(_causal_conv31(prev_b[:, ls], b_in[:, ls],
                                       lambda k, ls=ls: bcw_ref[0, k:k + 1, ls], v256(V256_B_CB, ls)))
        lnb = _layernorm(jnp.concatenate(conv, axis=1), v256(V256_B_LN_G), v256(V256_B_LN_B))
        br_ref[1, rs, :] = (lnb * _sigmoid(lnb)).astype(BF16)
        prev_b = jnp.concatenate([prev_b, b_in], axis=0)[-B_HPAD:]

        zc = z_ref[rs, 4 * BR:7 * BR]
        cc = zc[:, BR:2 * BR] * zc[:, 2 * BR:]
        ext_c = jnp.concatenate([prev_c, cc], axis=0)
        conv_c = v256(V256_C_W0 + C_KERNEL - 1) * cc
        for k in range(C_KERNEL - 1):
            off = C_HPAD - (C_KERNEL - 1) + k
            conv_c = conv_c + v256(V256_C_W0 + k) * _shift_up(ext_c, off)[:ch]
        br_ref[2, rs, :] = (zc[:, :BR] * conv_c).astype(BF16)
        prev_c = ext_c[-C_HPAD:]

        zd = z_ref[rs, 7 * BR:8 * BR]
        ext_d = jnp.concatenate([prev_d, zd], axis=0)
        wins = []
        for h in range(BR // LANES):
            e = ext_d[:, h * LANES:(h + 1) * LANES]
            s2 = e + _shift_down(e, 1)
            s4 = s2 + _shift_down(s2, 2)
            if h == 0:
                lo, hi = s2[D_HPAD:], s4[D_HPAD:]
            else:
                s8 = s4 + _shift_down(s4, 4)
                lo, hi = s8[D_HPAD:], s8[D_HPAD:] + s8[D_HPAD - 8:-8]
            wins.append(jnp.where(half_lo, lo, hi))
        win = jnp.concatenate(wins, axis=1)
        pos1 = pos0 + j * seg_len + t0 + 1 + lax.broadcasted_iota(jnp.int32, (ch, 1), 0)
        cnt = jnp.minimum(d_window, pos1).astype(F32)
        br_ref[3, rs, :] = (win / cnt - zd).astype(BF16)
        return prev_b, prev_c, ext_d[-D_HPAD:]

    def gate_task(n, cb, rs):
        cols = slice(cb * GATE_COLS, (cb + 1) * GATE_COLS)
        pre = _dot(xnb_ref[rs, :], wg_ref[0, n, :, cols])
        gate_ref[n, rs, cols] = _sigmoid(pre + v1024_ref[0, V1024_BGATE0 + n:V1024_BGATE0 + n + 1, cols])

    blk_seg = min(A_BLOCK, seg_len)
    ri = lax.broadcasted_iota(jnp.int32, (A_BLOCK, A_BLOCK), 0)
    ci = lax.broadcasted_iota(jnp.int32, (A_BLOCK, A_BLOCK), 1)
    keep = (ri // blk_seg == ci // blk_seg) & ((ci % blk_seg) // CHUNK <= (ri % blk_seg) // CHUNK)
    wsp = jnp.concatenate([jnp.where(keep, aw_ref[0, g], 0.0).astype(BF16) for g in range(A_GROUPS)], axis=0)

    def spatial_task(blk):
        rs = pl.ds(blk * A_BLOCK, A_BLOCK)
        sp4 = _dot(wsp, va_ref[rs, :].astype(BF16))
        sp = _lane_group_select(lane_group, [sp4[g * A_BLOCK:(g + 1) * A_BLOCK] for g in range(A_GROUPS)])
        br_ref[0, rs, :] = (ua_ref[rs, :] * (sp + abias_ref[0])).astype(BF16)

    def attn_task(s, r0, nr):
        rs = pl.ds(r0, nr)
        q = z_ref[rs, 8 * BR:9 * BR] * (X_HEAD_DIM ** -0.5)
        q4 = jnp.concatenate([jnp.where(lane_group == h, q, 0.0) for h in range(X_HEADS)], axis=0)
        sc = lax.dot_general(q4.astype(BF16), k_ref[s].astype(BF16), (((1,), (1,)), ((), ())),
                             preferred_element_type=F32)
        p = jnp.exp(sc - jnp.max(sc, axis=-1, keepdims=True))
        inv = 1.0 / jnp.sum(p, axis=-1, keepdims=True)
        o4 = _dot(p.astype(BF16), v_ref[s].astype(BF16)) * inv
        o = _lane_group_select(lane_group, [o4[h * nr:(h + 1) * nr] for h in range(X_HEADS)])
        br_ref[4, rs, :] = o.astype(BF16)

    sub_rows = rows // n_sub
    next_blk = 0
    hist = None
    for sub in range(n_sub):
        rsub = pl.ds(sub * sub_rows, sub_rows)

        xnb_ref[rsub, :] = _rms(x_ref[rsub, :], v1024_ref[0, V1024_NORM1:V1024_NORM1 + 1, :]).astype(BF16)
        z_ref[rsub, :] = _dot(xnb_ref[rsub, :], w_in_ref[0])

        if n_seg == 1:
            chunk_ids = [(0, c) for c in range(sub * sub_rows // ch, (sub + 1) * sub_rows // ch)]
            ar = min(ATTN_ROWS, sub_rows)
            attn_ids = [(0, r0, ar) for r0 in range(sub * sub_rows, (sub + 1) * sub_rows, ar)]
        else:
            chunk_ids = [(s, c) for s in range(n_seg) for c in range(seg_len // ch)]
            attn_ids = [(s, s * seg_len, seg_len) for s in range(n_seg)]
        gate_ids = [(n, cb, rsub) for n in range(N_BRANCH) for cb in range(D_MODEL // GATE_COLS)]
        slots = len(chunk_ids)
        for i, (s, c) in enumerate(chunk_ids):
            if c == 0:
                hist = (hist_b[s], hist_c[s], hist_d[s])
            hist = chunk_task(s, c, *hist)
            if c == seg_len // ch - 1:
                hist_b[s], hist_c[s], hist_d[s] = hist
            for ids in gate_ids[i * len(gate_ids) // slots:(i + 1) * len(gate_ids) // slots]:
                gate_task(*ids)
            while (next_blk + 1) * A_BLOCK <= s * seg_len + (c + 1) * ch:
                spatial_task(next_blk)
                next_blk += 1
            for ids in attn_ids[i * len(attn_ids) // slots:(i + 1) * len(attn_ids) // slots]:
                attn_task(*ids)

        merged = None
        for n in range(N_BRANCH):
            bn = br_ref[n, rsub, :]
            if n == 3:
                bn = (_dot(bn, dwbd_ref[0]) * v256(V256_D_SCALE)).astype(BF16)
            term = gate_ref[n, rsub, :] * _dot(bn, wbr_ref[0, n])
            merged = term if merged is None else merged + term
        x1_ref[rsub, :] = x_ref[rsub, :] + _dot(merged.astype(BF16), wout_ref[0])

    hbo_ref[...] = hist_b[...]
    hco_ref[...] = hist_c[...]
    hdo_ref[...] = hist_d[...]


def _resident(shape, layer):
    nd = len(shape)
    return pl.BlockSpec((1,) + tuple(shape[1:]), lambda *_: (layer,) + (0,) * (nd - 1),
                        pipeline_mode=pl.Buffered(1))


def _mixer(x, k_all, v_all, states, lw, layer, *, n_seq, seq_len, n_seg, seg_len, pos0, emit_va):
    has_state = states is not None
    rows = n_seg * seg_len
    tiles = seq_len // seg_len
    assert seq_len % seg_len == 0 and n_seq % n_seg == 0 and rows % A_BLOCK == 0
    assert tiles == 1 or n_seg == 1
    assert seg_len % A_BLOCK == 0 or A_BLOCK % seg_len == 0
    assert seg_len >= B_HPAD
    row_chunk = min(64, seg_len)
    n_sub = MIXER_SUBTILES if n_seg == 1 and seg_len % (MIXER_SUBTILES * A_BLOCK) == 0 else 1
    grid = (n_seq // n_seg, tiles)
    per_layer = n_seq // n_seg

    in_specs = [
        pl.BlockSpec((rows, D_MODEL), lambda b, j: (b * tiles + j, 0)),
        pl.BlockSpec((n_seg, N_MEM, BR), lambda b, j: (layer * per_layer + b, 0, 0)),
        pl.BlockSpec((n_seg, N_MEM, BR), lambda b, j: (layer * per_layer + b, 0, 0)),
    ]
    args = [x, k_all, v_all]
    if has_state:
        for st, hpad in zip(states, (B_HPAD, C_HPAD, D_HPAD)):
            in_specs.append(pl.BlockSpec((n_seg, hpad, BR), lambda b, j: (layer * per_layer + b, 0, 0)))
            args.append(st)
    names = ("v1024", "w_in", "v256", "a_w", "a_bias", "b_conv_w", "d_w_bd", "w_branch", "w_gate", "w_out")
    for name in names:
        in_specs.append(_resident(lw[name].shape, layer))
        args.append(lw[name])

    out_shape = [
        jax.ShapeDtypeStruct((n_seq * seq_len, D_MODEL), F32),
        jax.ShapeDtypeStruct((n_seq, B_HPAD, BR), F32),
        jax.ShapeDtypeStruct((n_seq, C_HPAD, BR), F32),
        jax.ShapeDtypeStruct((n_seq, D_HPAD, BR), F32),
    ]
    out_specs = [
        pl.BlockSpec((rows, D_MODEL), lambda b, j: (b * tiles + j, 0)),
        pl.BlockSpec((n_seg, B_HPAD, BR), lambda b, j: (b, 0, 0)),
        pl.BlockSpec((n_seg, C_HPAD, BR), lambda b, j: (b, 0, 0)),
        pl.BlockSpec((n_seg, D_HPAD, BR), lambda b, j: (b, 0, 0)),
    ]
    if emit_va:
        out_shape.append(jax.ShapeDtypeStruct((n_seq * seq_len, BR), F32))
        out_specs.append(pl.BlockSpec((rows, BR), lambda b, j: (b * tiles + j, 0)))

    scratch = [
        pltpu.VMEM((rows, D_MODEL), BF16),
        pltpu.VMEM((rows, W_IN_COLS), F32),
        pltpu.VMEM((n_seg, B_HPAD, BR), F32),
        pltpu.VMEM((n_seg, C_HPAD, BR), F32),
        pltpu.VMEM((n_seg, D_HPAD, BR), F32),
        pltpu.VMEM((N_BRANCH, rows, BR), BF16),
        pltpu.VMEM((rows, BR), F32),
        pltpu.VMEM((rows, BR), F32),
        pltpu.VMEM((N_BRANCH, rows, D_MODEL), F32),
    ]
    kern = functools.partial(_mixer_kernel, n_seg=n_seg, seg_len=seg_len, has_state=has_state,
                             pos0=pos0, emit_va=emit_va, row_chunk=row_chunk, n_sub=n_sub)
    return pl.pallas_call(
        kern, grid=grid, in_specs=in_specs, out_specs=out_specs, out_shape=out_shape,
        scratch_shapes=scratch, name="mixer_state" if has_state else "mixer_prompt",
        compiler_params=pltpu.CompilerParams(dimension_semantics=("arbitrary", "arbitrary"),
                                             vmem_limit_bytes=VMEM_LIMIT),
    )(*args)


def _ffn_kernel(x_ref, g_ref, wg_ref, wu_ref, wd_ref, gf_ref, o_ref, *, final_norm):
    x = x_ref[...]
    hn = _rms(x, g_ref[0]).astype(BF16)
    y = x
    for c0 in range(0, D_FF, FF_CHUNK):
        c1 = min(c0 + FF_CHUNK, D_FF)
        hg = _dot(hn, wg_ref[0, :, c0:c1])
        act = (hg * _sigmoid(hg) * _dot(hn, wu_ref[0, :, c0:c1])).astype(BF16)
        y = y + _dot(act, wd_ref[0, c0:c1, :])
    if final_norm:
        y = _rms(y, gf_ref[...])
    o_ref[...] = y


def _ffn(x, lw, layer, final_g, *, row_tile, final_norm):
    n_rows = x.shape[0]
    assert n_rows % row_tile == 0
    names = ("norm2_g", "w_ffn_gate", "w_ffn_up", "w_ffn_down")
    in_specs = [pl.BlockSpec((row_tile, D_MODEL), lambda i: (i, 0))]
    in_specs += [_resident(lw[n].shape, layer) for n in names]
    in_specs.append(pl.BlockSpec((1, D_MODEL), lambda i: (0, 0)))
    return pl.pallas_call(
        functools.partial(_ffn_kernel, final_norm=final_norm),
        grid=(n_rows // row_tile,), in_specs=in_specs,
        out_specs=pl.BlockSpec((row_tile, D_MODEL), lambda i: (i, 0)),
        out_shape=jax.ShapeDtypeStruct((n_rows, D_MODEL), F32), name="ffn",
        compiler_params=pltpu.CompilerParams(dimension_semantics=("arbitrary",),
                                             vmem_limit_bytes=VMEM_LIMIT),
    )(x, *[lw[n] for n in names], final_g)


def _memkv_kernel(m_ref, g_ref, w_ref, k_ref, v_ref):
    kv = _dot(_rms(m_ref[...], g_ref[0]).astype(BF16), w_ref[0])
    k_ref[0] = kv[:, :BR]
    v_ref[0] = kv[:, BR:]


def _memory_kv(mem_rows, mem_norm_g, w_mem_kv, *, row_tile):
    n_rows = mem_rows.shape[0]
    assert n_rows % row_tile == 0
    tiles = n_rows // row_tile
    out = jax.ShapeDtypeStruct((DEPTH, n_rows, BR), F32)
    return pl.pallas_call(
        _memkv_kernel, grid=(DEPTH, tiles),
        in_specs=[pl.BlockSpec((row_tile, D_MODEL), lambda l, i: (i, 0)),
                  pl.BlockSpec((1, 1, D_MODEL), lambda l, i: (l, 0, 0)),
                  pl.BlockSpec((1, D_MODEL, 2 * BR), lambda l, i: (l, 0, 0))],
        out_specs=[pl.BlockSpec((1, row_tile, BR), lambda l, i: (l, i, 0))] * 2,
        out_shape=[out, out], name="memory_kv",
        compiler_params=pltpu.CompilerParams(dimension_semantics=("arbitrary", "arbitrary"),
                                             vmem_limit_bytes=VMEM_LIMIT),
    )(mem_rows, mem_norm_g, w_mem_kv)


MIXER_ROWS = 512
FFN_ROWS = 1024


def _mixer_prompt_call(x, k_all, v_all, lw, layer, batch, seq):
    return _mixer(x, k_all, v_all, None, lw, layer, n_seq=batch, seq_len=seq, n_seg=1,
                  seg_len=min(MIXER_ROWS, seq), pos0=0, emit_va=False)


def _mixer_state_call(x, k_all, v_all, states, lw, layer, dec_batch, dec_seq):
    return _mixer(x, k_all, v_all, states, lw, layer, n_seq=dec_batch, seq_len=dec_seq,
                  n_seg=8, seg_len=dec_seq, pos0=PAST_LEN, emit_va=True)


def _ffn_call(x, lw, layer, final_g, final_norm):
    return _ffn(x, lw, layer, final_g, row_tile=min(FFN_ROWS, x.shape[0]), final_norm=final_norm)


def _pad_rows_front(a, total):
    pad = total - a.shape[-2]
    return jnp.pad(a, [(0, 0)] * (a.ndim - 2) + [(pad, 0), (0, 0)])


def _spatial_layout(a_ws, a_bs, blk_seg):
    reps = A_BLOCK // blk_seg
    w = jnp.tile(a_ws[:, :, :blk_seg, :blk_seg], (1, 1, reps, reps))
    bias = jnp.repeat(jnp.swapaxes(a_bs, 1, 2), BR // A_GROUPS, axis=2)
    bias = jnp.tile(bias[:, :blk_seg, :], (1, reps, 1))
    return w, bias


def _layout_states(state_conv_b, state_conv_c, state_pool_d):
    return tuple(_pad_rows_front(st, hpad).reshape(-1, hpad, BR)
                 for st, hpad in ((state_conv_b, B_HPAD), (state_conv_c, C_HPAD), (state_pool_d, D_HPAD)))


def _layout_params(seq, dec_seq, norm1_g, w_in, a_ln_g, a_ln_b, a_ws, a_bs, b_conv_w, b_conv_b, b_ln_g,
                   b_ln_b, c_conv_w, d_w, d_scale, w_branch, w_gate, b_gate, w_out, norm2_g, w_ffn_gate,
                   w_ffn_up, w_ffn_down):
    depth = w_in.shape[0]
    zeros256 = jnp.zeros((depth, 16 - 6 - C_KERNEL, BR), F32)
    v256 = jnp.concatenate([a_ln_g[:, None], a_ln_b[:, None], b_conv_b[:, None], b_ln_g[:, None],
                            b_ln_b[:, None], d_scale[:, None], c_conv_w, zeros256], axis=1)
    v1024 = jnp.concatenate([norm1_g[:, None], b_gate, jnp.zeros((depth, 2, D_MODEL), F32)], axis=1)
    eye = jnp.eye(len(D_WINDOWS), dtype=F32)
    d_w_bd = jnp.einsum('lgcd,gh->lgchd', d_w, eye).reshape(depth, BR, BR).astype(BF16)
    lw = {
        "v1024": v1024, "v256": v256, "w_in": w_in.astype(BF16), "b_conv_w": b_conv_w,
        "d_w_bd": d_w_bd, "w_branch": w_branch.astype(BF16), "w_gate": w_gate.astype(BF16),
        "w_out": w_out.astype(BF16), "norm2_g": norm2_g[:, None],
        "w_ffn_gate": w_ffn_gate.astype(BF16), "w_ffn_up": w_ffn_up.astype(BF16),
        "w_ffn_down": w_ffn_down.astype(BF16),
    }
    lw_p = dict(lw)
    lw_p["a_w"], lw_p["a_bias"] = _spatial_layout(a_ws, a_bs, min(A_BLOCK, seq))
    lw_s = dict(lw)
    lw_s["a_w"], lw_s["a_bias"] = _spatial_layout(a_ws, a_bs, min(A_BLOCK, dec_seq))
    return lw, lw_p, lw_s


def kernel(x_prompt, x_sample, mem_prompt, cache_mem_k, cache_mem_v, state_conv_b, state_conv_c, state_pool_d, norm1_g, mem_norm_g, w_in, a_ln_g, a_ln_b, a_ws, a_bs, b_conv_w, b_conv_b, b_ln_g, b_ln_b, c_conv_w, d_w, d_scale, w_mem_kv, w_branch, w_gate, b_gate, w_out, norm2_g, w_ffn_gate, w_ffn_up, w_ffn_down, final_norm_g):
    batch, seq, _ = x_prompt.shape
    dec_batch, dec_seq, _ = x_sample.shape
    depth = w_in.shape[0]
    assert depth == DEPTH
    lw, lw_p, lw_s = _layout_params(
        seq, dec_seq, norm1_g, w_in, a_ln_g, a_ln_b, a_ws, a_bs, b_conv_w, b_conv_b, b_ln_g, b_ln_b,
        c_conv_w, d_w, d_scale, w_branch, w_gate, b_gate, w_out, norm2_g, w_ffn_gate, w_ffn_up, w_ffn_down)
    final_g = final_norm_g[None]

    mk_p, mv_p = _memory_kv(mem_prompt.reshape(batch * N_MEM, D_MODEL), mem_norm_g[:, None],
                            w_mem_kv.astype(BF16), row_tile=1024)
    k_p = mk_p.reshape(depth * batch, N_MEM, BR)
    v_p = mv_p.reshape(depth * batch, N_MEM, BR)
    k_s = cache_mem_k.reshape(depth * dec_batch, N_MEM, BR)
    v_s = cache_mem_v.reshape(depth * dec_batch, N_MEM, BR)
    states = _layout_states(state_conv_b, state_conv_c, state_pool_d)

    xp = x_prompt.reshape(batch * seq, D_MODEL)
    xs = x_sample.reshape(dec_batch * dec_seq, D_MODEL)
    cb_p, cc_p, pd_p, av_s, cb_s, cc_s, pd_s = [], [], [], [], [], [], []
    for l in range(depth):
        last = l == depth - 1
        xp, hb, hc, hd = _mixer_prompt_call(xp, k_p, v_p, lw_p, l, batch, seq)
        xp = _ffn_call(xp, lw, l, final_g, last)
        cb_p.append(hb); cc_p.append(hc); pd_p.append(hd)
        xs, hb, hc, hd, va = _mixer_state_call(xs, k_s, v_s, states, lw_s, l, dec_batch, dec_seq)
        xs = _ffn_call(xs, lw, l, final_g, last)
        cb_s.append(hb); cc_s.append(hc); pd_s.append(hd)
        av_s.append(va.reshape(dec_batch, dec_seq, BR))

    def tails(parts, hpad, hist):
        return jnp.stack(parts)[:, :, hpad - hist:, :]

    kv_shape = (depth, batch, N_MEM, X_HEADS, X_HEAD_DIM)
    return (xp.reshape(batch, seq, D_MODEL), xs.reshape(dec_batch, dec_seq, D_MODEL),
            mk_p.reshape(kv_shape), mv_p.reshape(kv_shape),
            tails(cb_p, B_HPAD, B_KERNEL - 1), tails(cc_p, C_HPAD, C_KERNEL - 1), tails(pd_p, D_HPAD, D_HIST),
            jnp.stack(av_s),
            tails(cb_s, B_HPAD, B_KERNEL - 1), tails(cc_s, C_HPAD, C_KERNEL - 1), tails(pd_s, D_HPAD, D_HIST))
```

```python
import functools

import jax
import jax.numpy as jnp
from jax import lax
from jax.experimental import pallas as pl
from jax.experimental.pallas import tpu as pltpu

D_MODEL = 1024
DEPTH = 4
CHUNK = 64
PAST_LEN = 2048
BR = D_MODEL // 4
N_BRANCH = 5
A_BLOCK = 128
A_GROUPS = 4
B_KERNEL = 31
C_KERNEL = 3
D_WINDOWS = (2, 4, 8, 16)
D_HIST = max(D_WINDOWS) - 1
N_MEM = 256
X_HEADS = 4
X_HEAD_DIM = BR // X_HEADS
D_FF = ((8 * D_MODEL // 3 + 255) // 256) * 256
W_IN_COLS = 9 * BR
EPS = 1e-6

SUBLANES = 8
LANES = 128
B_HPAD = 32
C_HPAD = 8
D_HPAD = 16
VMEM_LIMIT = 56 * 1024 * 1024
MIXER_SUBTILES = 1
FF_CHUNK = 1024
ATTN_ROWS = 512

BF16 = jnp.bfloat16
F32 = jnp.float32

V256_A_LN_G, V256_A_LN_B, V256_B_CB, V256_B_LN_G, V256_B_LN_B, V256_D_SCALE, V256_C_W0 = range(7)
V1024_NORM1, V1024_BGATE0 = 0, 1


def _dot(a, b):
    return jnp.dot(a, b, preferred_element_type=F32)


def _sigmoid(x):
    return 0.5 * jnp.tanh(0.5 * x) + 0.5


def _rms(x, g):
    return x * lax.rsqrt(jnp.mean(x * x, axis=-1, keepdims=True) + EPS) * g


def _layernorm(x, g, b):
    mu = jnp.mean(x, axis=-1, keepdims=True)
    xc = x - mu
    var = jnp.mean(xc * xc, axis=-1, keepdims=True)
    return xc * lax.rsqrt(var + EPS) * g + b


def _lane_group_select(lane_group, vals):
    out = vals[-1]
    for g in range(len(vals) - 2, -1, -1):
        out = jnp.where(lane_group == g, vals[g], out)
    return out


def _shift_up(x, n):
    return x if n == 0 else pltpu.roll(x, x.shape[0] - n, axis=0)


def _shift_down(x, n):
    return x if n == 0 else pltpu.roll(x, n, axis=0)


def _causal_conv31(prev, cur, w_ref, bias):
    ch = cur.shape[0]
    ext = jnp.concatenate([prev, cur], axis=0)
    lead = B_HPAD - (B_KERNEL - 1)
    acc = bias + w_ref(B_KERNEL - 1) * cur
    for b in range(SUBLANES):
        shifted = _shift_up(ext, b)
        for a in range(B_HPAD // SUBLANES):
            k = a * SUBLANES + b - lead
            if 0 <= k < B_KERNEL - 1:
                acc = acc + w_ref(k) * shifted[a * SUBLANES:a * SUBLANES + ch]
    return acc


def _mixer_kernel(*refs, n_seg, seg_len, has_state, pos0, emit_va, row_chunk, n_sub):
    it = iter(refs)
    x_ref, k_ref, v_ref = next(it), next(it), next(it)
    if has_state:
        hb_ref, hc_ref, hd_ref = next(it), next(it), next(it)
    v1024_ref, w_in_ref, v256_ref, aw_ref, abias_ref = (next(it) for _ in range(5))
    bcw_ref, dwbd_ref, wbr_ref, wg_ref, wout_ref = (next(it) for _ in range(5))
    x1_ref, hbo_ref, hco_ref, hdo_ref = (next(it) for _ in range(4))
    vao_ref = next(it) if emit_va else None
    xnb_ref, z_ref, hist_b, hist_c, hist_d, br_ref, ua_ref, va_ref = (next(it) for _ in range(8))

    rows = n_seg * seg_len
    j = pl.program_id(1)

    def v256(r, lanes=slice(None)):
        return v256_ref[0, r:r + 1, lanes]

    @pl.when(j == 0)
    def _():
        if has_state:
            hist_b[...] = hb_ref[...]
            hist_c[...] = hc_ref[...]
            hist_d[...] = hd_ref[...]
        else:
            hist_b[...] = jnp.zeros(hist_b.shape, F32)
            hist_c[...] = jnp.zeros(hist_c.shape, F32)
            hist_d[...] = jnp.zeros(hist_d.shape, F32)

    lane_group = lax.broadcasted_iota(jnp.int32, (1, BR), 1) // (BR // 4)
    d_window = _lane_group_select(lane_group, [jnp.full((1, BR), w, jnp.int32) for w in D_WINDOWS])
    half_lo = lax.broadcasted_iota(jnp.int32, (1, LANES), 1) < LANES // 2

    ch = row_chunk

    def chunk_task(s, c, prev_b, prev_c, prev_d):
        t0 = c * ch
        rs = pl.ds(s * seg_len + t0, ch)

        ga = jax.nn.gelu(z_ref[rs, 0:2 * BR])
        ua_ref[rs, :] = ga[:, :BR]
        va = _layernorm(ga[:, BR:], v256(V256_A_LN_G), v256(V256_A_LN_B))
        va_ref[rs, :] = va
        if emit_va:
            vao_ref[rs, :] = va

        zb = z_ref[rs, 2 * BR:4 * BR]
        b_in = zb[:, :BR] * _sigmoid(zb[:, BR:])
        conv = []
        for h in range(BR // LANES):
            ls = slice(h * LANES, (h + 1) * LANES)
            conv.append(_causal_conv31(prev_b[:, ls], b_in[:, ls],
                                       lambda k, ls=ls: bcw_ref[0, k:k + 1, ls], v256(V256_B_CB, ls)))
        lnb = _layernorm(jnp.concatenate(conv, axis=1), v256(V256_B_LN_G), v256(V256_B_LN_B))
        br_ref[1, rs, :] = (lnb * _sigmoid(lnb)).astype(BF16)
        prev_b = jnp.concatenate([prev_b, b_in], axis=0)[-B_HPAD:]

        zc = z_ref[rs, 4 * BR:7 * BR]
        cc = zc[:, BR:2 * BR] * zc[:, 2 * BR:]
        ext_c = jnp.concatenate([prev_c, cc], axis=0)
        conv_c = v256(V256_C_W0 + C_KERNEL - 1) * cc
        for k in range(C_KERNEL - 1):
            off = C_HPAD - (C_KERNEL - 1) + k
            conv_c = conv_c + v256(V256_C_W0 + k) * _shift_up(ext_c, off)[:ch]
        br_ref[2, rs, :] = (zc[:, :BR] * conv_c).astype(BF16)
        prev_c = ext_c[-C_HPAD:]

        zd = z_ref[rs, 7 * BR:8 * BR]
        ext_d = jnp.concatenate([prev_d, zd], axis=0)
        wins = []
        for h in range(BR // LANES):
            e = ext_d[:, h * LANES:(h + 1) * LANES]
            s2 = e + _shift_down(e, 1)
            s4 = s2 + _shift_down(s2, 2)
            if h == 0:
                lo, hi = s2[D_HPAD:], s4[D_HPAD:]
            else:
                s8 = s4 + _shift_down(s4, 4)
                lo, hi = s8[D_HPAD:], s8[D_HPAD:] + s8[D_HPAD - 8:-8]
            wins.append(jnp.where(half_lo, lo, hi))
        win = jnp.concatenate(wins, axis=1)
        pos1 = pos0 + j * seg_len + t0 + 1 + lax.broadcasted_iota(jnp.int32, (ch, 1), 0)
        cnt = jnp.minimum(d_window, pos1).astype(F32)
        br_ref[3, rs, :] = (win / cnt - zd).astype(BF16)
        return prev_b, prev_c, ext_d[-D_HPAD:]

    blk_seg = min(A_BLOCK, seg_len)
    ri = lax.broadcasted_iota(jnp.int32, (A_BLOCK, A_BLOCK), 0)
    ci = lax.broadcasted_iota(jnp.int32, (A_BLOCK, A_BLOCK), 1)
    keep = (ri // blk_seg == ci // blk_seg) & ((ci % blk_seg) // CHUNK <= (ri % blk_seg) // CHUNK)
    wsp = jnp.concatenate([jnp.where(keep, aw_ref[0, g], 0.0).astype(BF16) for g in range(A_GROUPS)], axis=0)

    def spatial_task(blk):
        rs = pl.ds(blk * A_BLOCK, A_BLOCK)
        sp4 = _dot(wsp, va_ref[rs, :].astype(BF16))
        sp = _lane_group_select(lane_group, [sp4[g * A_BLOCK:(g + 1) * A_BLOCK] for g in range(A_GROUPS)])
        br_ref[0, rs, :] = (ua_ref[rs, :] * (sp + abias_ref[0])).astype(BF16)

    def attn_task(s, r0, nr):
        rs = pl.ds(r0, nr)
        q = z_ref[rs, 8 * BR:9 * BR] * (X_HEAD_DIM ** -0.5)
        q4 = jnp.concatenate([jnp.where(lane_group == h, q, 0.0) for h in range(X_HEADS)], axis=0)
        sc = lax.dot_general(q4.astype(BF16), k_ref[s].astype(BF16), (((1,), (1,)), ((), ())),
                             preferred_element_type=F32)
        p = jnp.exp(sc - jnp.max(sc, axis=-1, keepdims=True))
        inv = 1.0 / jnp.sum(p, axis=-1, keepdims=True)
        o4 = _dot(p.astype(BF16), v_ref[s].astype(BF16)) * inv
        o = _lane_group_select(lane_group, [o4[h * nr:(h + 1) * nr] for h in range(X_HEADS)])
        br_ref[4, rs, :] = o.astype(BF16)

    sub_rows = rows // n_sub
    next_blk = 0
    hist = None
    for sub in range(n_sub):
        rsub = pl.ds(sub * sub_rows, sub_rows)

        xnb_ref[rsub, :] = _rms(x_ref[rsub, :], v1024_ref[0, V1024_NORM1:V1024_NORM1 + 1, :]).astype(BF16)
        z_ref[rsub, :] = _dot(xnb_ref[rsub, :], w_in_ref[0])

        if n_seg == 1:
            chunk_ids = [(0, c) for c in range(sub * sub_rows // ch, (sub + 1) * sub_rows // ch)]
            ar = min(ATTN_ROWS, sub_rows)
            attn_ids = [(0, r0, ar) for r0 in range(sub * sub_rows, (sub + 1) * sub_rows, ar)]
        else:
            chunk_ids = [(s, c) for s in range(n_seg) for c in range(seg_len // ch)]
            attn_ids = [(s, s * seg_len, seg_len) for s in range(n_seg)]
        slots = len(chunk_ids)
        for i, (s, c) in enumerate(chunk_ids):
            if c == 0:
                hist = (hist_b[s], hist_c[s], hist_d[s])
            hist = chunk_task(s, c, *hist)
            if c == seg_len // ch - 1:
                hist_b[s], hist_c[s], hist_d[s] = hist
            while (next_blk + 1) * A_BLOCK <= s * seg_len + (c + 1) * ch:
                spatial_task(next_blk)
                next_blk += 1
            for ids in attn_ids[i * len(attn_ids) // slots:(i + 1) * len(attn_ids) // slots]:
                attn_task(*ids)

        merged = None
        for n in range(N_BRANCH):
            bn = br_ref[n, rsub, :]
            if n == 3:
                bn = (_dot(bn, dwbd_ref[0]) * v256(V256_D_SCALE)).astype(BF16)
            gate = _sigmoid(_dot(xnb_ref[rsub, :], wg_ref[0, n])
                            + v1024_ref[0, V1024_BGATE0 + n:V1024_BGATE0 + n + 1, :])
            term = gate * _dot(bn, wbr_ref[0, n])
            merged = term if merged is None else merged + term
        x1_ref[rsub, :] = x_ref[rsub, :] + _dot(merged.astype(BF16), wout_ref[0])

    hbo_ref[...] = hist_b[...]
    hco_ref[...] = hist_c[...]
    hdo_ref[...] = hist_d[...]


def _resident(shape, layer):
    nd = len(shape)
    return pl.BlockSpec((1,) + tuple(shape[1:]), lambda *_: (layer,) + (0,) * (nd - 1),
                        pipeline_mode=pl.Buffered(1))


def _mixer(x, k_all, v_all, states, lw, layer, *, n_seq, seq_len, n_seg, seg_len, pos0, emit_va):
    has_state = states is not None
    rows = n_seg * seg_len
    tiles = seq_len // seg_len
    assert seq_len % seg_len == 0 and n_seq % n_seg == 0 and rows % A_BLOCK == 0
    assert tiles == 1 or n_seg == 1
    assert seg_len % A_BLOCK == 0 or A_BLOCK % seg_len == 0
    assert seg_len >= B_HPAD
    row_chunk = min(64, seg_len)
    n_sub = MIXER_SUBTILES if n_seg == 1 and seg_len % (MIXER_SUBTILES * A_BLOCK) == 0 else 1
    grid = (n_seq // n_seg, tiles)
    per_layer = n_seq // n_seg

    in_specs = [
        pl.BlockSpec((rows, D_MODEL), lambda b, j: (b * tiles + j, 0)),
        pl.BlockSpec((n_seg, N_MEM, BR), lambda b, j: (layer * per_layer + b, 0, 0)),
        pl.BlockSpec((n_seg, N_MEM, BR), lambda b, j: (layer * per_layer + b, 0, 0)),
    ]
    args = [x, k_all, v_all]
    if has_state:
        for st, hpad in zip(states, (B_HPAD, C_HPAD, D_HPAD)):
            in_specs.append(pl.BlockSpec((n_seg, hpad, BR), lambda b, j: (layer * per_layer + b, 0, 0)))
            args.append(st)
    names = ("v1024", "w_in", "v256", "a_w", "a_bias", "b_conv_w", "d_w_bd", "w_branch", "w_gate", "w_out")
    for name in names:
        in_specs.append(_resident(lw[name].shape, layer))
        args.append(lw[name])

    out_shape = [
        jax.ShapeDtypeStruct((n_seq * seq_len, D_MODEL), F32),
        jax.ShapeDtypeStruct((n_seq, B_HPAD, BR), F32),
        jax.ShapeDtypeStruct((n_seq, C_HPAD, BR), F32),
        jax.ShapeDtypeStruct((n_seq, D_HPAD, BR), F32),
    ]
    out_specs = [
        pl.BlockSpec((rows, D_MODEL), lambda b, j: (b * tiles + j, 0)),
        pl.BlockSpec((n_seg, B_HPAD, BR), lambda b, j: (b, 0, 0)),
        pl.BlockSpec((n_seg, C_HPAD, BR), lambda b, j: (b, 0, 0)),
        pl.BlockSpec((n_seg, D_HPAD, BR), lambda b, j: (b, 0, 0)),
    ]
    if emit_va:
        out_shape.append(jax.ShapeDtypeStruct((n_seq * seq_len, BR), F32))
        out_specs.append(pl.BlockSpec((rows, BR), lambda b, j: (b * tiles + j, 0)))

    scratch = [
        pltpu.VMEM((rows, D_MODEL), BF16),
        pltpu.VMEM((rows, W_IN_COLS), F32),
        pltpu.VMEM((n_seg, B_HPAD, BR), F32),
        pltpu.VMEM((n_seg, C_HPAD, BR), F32),
        pltpu.VMEM((n_seg, D_HPAD, BR), F32),
        pltpu.VMEM((N_BRANCH, rows, BR), BF16),
        pltpu.VMEM((rows, BR), F32),
        pltpu.VMEM((rows, BR), F32),
    ]
    kern = functools.partial(_mixer_kernel, n_seg=n_seg, seg_len=seg_len, has_state=has_state,
                             pos0=pos0, emit_va=emit_va, row_chunk=row_chunk, n_sub=n_sub)
    return pl.pallas_call(
        kern, grid=grid, in_specs=in_specs, out_specs=out_specs, out_shape=out_shape,
        scratch_shapes=scratch, name="mixer_state" if has_state else "mixer_prompt",
        compiler_params=pltpu.CompilerParams(dimension_semantics=("arbitrary", "arbitrary"),
                                             vmem_limit_bytes=VMEM_LIMIT),
    )(*args)


def _ffn_kernel(x_ref, g_ref, wg_ref, wu_ref, wd_ref, gf_ref, o_ref, *, final_norm):
    x = x_ref[...]
    hn = _rms(x, g_ref[0]).astype(BF16)
    y = x
    for c0 in range(0, D_FF, FF_CHUNK):
        c1 = min(c0 + FF_CHUNK, D_FF)
        hg = _dot(hn, wg_ref[0, :, c0:c1])
        act = (hg * _sigmoid(hg) * _dot(hn, wu_ref[0, :, c0:c1])).astype(BF16)
        y = y + _dot(act, wd_ref[0, c0:c1, :])
    if final_norm:
        y = _rms(y, gf_ref[...])
    o_ref[...] = y


def _ffn(x, lw, layer, final_g, *, row_tile, final_norm):
    n_rows = x.shape[0]
    assert n_rows % row_tile == 0
    names = ("norm2_g", "w_ffn_gate", "w_ffn_up", "w_ffn_down")
    in_specs = [pl.BlockSpec((row_tile, D_MODEL), lambda i: (i, 0))]
    in_specs += [_resident(lw[n].shape, layer) for n in names]
    in_specs.append(pl.BlockSpec((1, D_MODEL), lambda i: (0, 0)))
    return pl.pallas_call(
        functools.partial(_ffn_kernel, final_norm=final_norm),
        grid=(n_rows // row_tile,), in_specs=in_specs,
        out_specs=pl.BlockSpec((row_tile, D_MODEL), lambda i: (i, 0)),
        out_shape=jax.ShapeDtypeStruct((n_rows, D_MODEL), F32), name="ffn",
        compiler_params=pltpu.CompilerParams(dimension_semantics=("arbitrary",),
                                             vmem_limit_bytes=VMEM_LIMIT),
    )(x, *[lw[n] for n in names], final_g)


def _memkv_kernel(m_ref, g_ref, w_ref, k_ref, v_ref):
    kv = _dot(_rms(m_ref[...], g_ref[0]).astype(BF16), w_ref[0])
    k_ref[0] = kv[:, :BR]
    v_ref[0] = kv[:, BR:]


def _memory_kv(mem_rows, mem_norm_g, w_mem_kv, *, row_tile):
    n_rows = mem_rows.shape[0]
    assert n_rows % row_tile == 0
    tiles = n_rows // row_tile
    out = jax.ShapeDtypeStruct((DEPTH, n_rows, BR), F32)
    return pl.pallas_call(
        _memkv_kernel, grid=(DEPTH, tiles),
        in_specs=[pl.BlockSpec((row_tile, D_MODEL), lambda l, i: (i, 0)),
                  pl.BlockSpec((1, 1, D_MODEL), lambda l, i: (l, 0, 0)),
                  pl.BlockSpec((1, D_MODEL, 2 * BR), lambda l, i: (l, 0, 0))],
        out_specs=[pl.BlockSpec((1, row_tile, BR), lambda l, i: (l, i, 0))] * 2,
        out_shape=[out, out], name="memory_kv",
        compiler_params=pltpu.CompilerParams(dimension_semantics=("arbitrary", "arbitrary"),
                                             vmem_limit_bytes=VMEM_LIMIT),
    )(mem_rows, mem_norm_g, w_mem_kv)


MIXER_ROWS = 512
FFN_ROWS = 1024


def _mixer_prompt_call(x, k_all, v_all, lw, layer, batch, seq):
    return _mixer(x, k_all, v_all, None, lw, layer, n_seq=batch, seq_len=seq, n_seg=1,
                  seg_len=min(MIXER_ROWS, seq), pos0=0, emit_va=False)


def _mixer_state_call(x, k_all, v_all, states, lw, layer, dec_batch, dec_seq):
    return _mixer(x, k_all, v_all, states, lw, layer, n_seq=dec_batch, seq_len=dec_seq,
                  n_seg=8, seg_len=dec_seq, pos0=PAST_LEN, emit_va=True)


def _ffn_call(x, lw, layer, final_g, final_norm):
    return _ffn(x, lw, layer, final_g, row_tile=min(FFN_ROWS, x.shape[0]), final_norm=final_norm)


def _pad_rows_front(a, total):
    pad = total - a.shape[-2]
    return jnp.pad(a, [(0, 0)] * (a.ndim - 2) + [(pad, 0), (0, 0)])


def _spatial_layout(a_ws, a_bs, blk_seg):
    reps = A_BLOCK // blk_seg
    w = jnp.tile(a_ws[:, :, :blk_seg, :blk_seg], (1, 1, reps, reps))
    bias = jnp.repeat(jnp.swapaxes(a_bs, 1, 2), BR // A_GROUPS, axis=2)
    bias = jnp.tile(bias[:, :blk_seg, :], (1, reps, 1))
    return w, bias


def _layout_states(state_conv_b, state_conv_c, state_pool_d):
    return tuple(_pad_rows_front(st, hpad).reshape(-1, hpad, BR)
                 for st, hpad in ((state_conv_b, B_HPAD), (state_conv_c, C_HPAD), (state_pool_d, D_HPAD)))


def _layout_params(seq, dec_seq, norm1_g, w_in, a_ln_g, a_ln_b, a_ws, a_bs, b_conv_w, b_conv_b, b_ln_g,
                   b_ln_b, c_conv_w, d_w, d_scale, w_branch, w_gate, b_gate, w_out, norm2_g, w_ffn_gate,
                   w_ffn_up, w_ffn_down):
    depth = w_in.shape[0]
    zeros256 = jnp.zeros((depth, 16 - 6 - C_KERNEL, BR), F32)
    v256 = jnp.concatenate([a_ln_g[:, None], a_ln_b[:, None], b_conv_b[:, None], b_ln_g[:, None],
                            b_ln_b[:, None], d_scale[:, None], c_conv_w, zeros256], axis=1)
    v1024 = jnp.concatenate([norm1_g[:, None], b_gate, jnp.zeros((depth, 2, D_MODEL), F32)], axis=1)
    eye = jnp.eye(len(D_WINDOWS), dtype=F32)
    d_w_bd = jnp.einsum('lgcd,gh->lgchd', d_w, eye).reshape(depth, BR, BR).astype(BF16)
    lw = {
        "v1024": v1024, "v256": v256, "w_in": w_in.astype(BF16), "b_conv_w": b_conv_w,
        "d_w_bd": d_w_bd, "w_branch": w_branch.astype(BF16), "w_gate": w_gate.astype(BF16),
        "w_out": w_out.astype(BF16), "norm2_g": norm2_g[:, None],
        "w_ffn_gate": w_ffn_gate.astype(BF16), "w_ffn_up": w_ffn_up.astype(BF16),
        "w_ffn_down": w_ffn_down.astype(BF16),
    }
    lw_p = dict(lw)
    lw_p["a_w"], lw_p["a_bias"] = _spatial_layout(a_ws, a_bs, min(A_BLOCK, seq))
    lw_s = dict(lw)
    lw_s["a_w"], lw_s["a_bias"] = _spatial_layout(a_ws, a_bs, min(A_BLOCK, dec_seq))
    return lw, lw_p, lw_s


def kernel(x_prompt, x_sample, mem_prompt, cache_mem_k, cache_mem_v, state_conv_b, state_conv_c, state_pool_d, norm1_g, mem_norm_g, w_in, a_ln_g, a_ln_b, a_ws, a_bs, b_conv_w, b_conv_b, b_ln_g, b_ln_b, c_conv_w, d_w, d_scale, w_mem_kv, w_branch, w_gate, b_gate, w_out, norm2_g, w_ffn_gate, w_ffn_up, w_ffn_down, final_norm_g):
    batch, seq, _ = x_prompt.shape
    dec_batch, dec_seq, _ = x_sample.shape
    depth = w_in.shape[0]
    assert depth == DEPTH
    lw, lw_p, lw_s = _layout_params(
        seq, dec_seq, norm1_g, w_in, a_ln_g, a_ln_b, a_ws, a_bs, b_conv_w, b_conv_b, b_ln_g, b_ln_b,
        c_conv_w, d_w, d_scale, w_branch, w_gate, b_gate, w_out, norm2_g, w_ffn_gate, w_ffn_up, w_ffn_down)
    final_g = final_norm_g[None]

    mk_p, mv_p = _memory_kv(mem_prompt.reshape(batch * N_MEM, D_MODEL), mem_norm_g[:, None],
                            w_mem_kv.astype(BF16), row_tile=1024)
    k_p = mk_p.reshape(depth * batch, N_MEM, BR)
    v_p = mv_p.reshape(depth * batch, N_MEM, BR)
    k_s = cache_mem_k.reshape(depth * dec_batch, N_MEM, BR)
    v_s = cache_mem_v.reshape(depth * dec_batch, N_MEM, BR)
    states = _layout_states(state_conv_b, state_conv_c, state_pool_d)

    xp = x_prompt.reshape(batch * seq, D_MODEL)
    xs = x_sample.reshape(dec_batch * dec_seq, D_MODEL)
    cb_p, cc_p, pd_p, av_s, cb_s, cc_s, pd_s = [], [], [], [], [], [], []
    for l in range(depth):
        last = l == depth - 1
        xp, hb, hc, hd = _mixer_prompt_call(xp, k_p, v_p, lw_p, l, batch, seq)
        xp = _ffn_call(xp, lw, l, final_g, last)
        cb_p.append(hb); cc_p.append(hc); pd_p.append(hd)
        xs, hb, hc, hd, va = _mixer_state_call(xs, k_s, v_s, states, lw_s, l, dec_batch, dec_seq)
        xs = _ffn_call(xs, lw, l, final_g, last)
        cb_s.append(hb); cc_s.append(hc); pd_s.append(hd)
        av_s.append(va.reshape(dec_batch, dec_seq, BR))

    def tails(parts, hpad, hist):
        return jnp.stack(parts)[:, :, hpad - hist:, :]

    kv_shape = (depth, batch, N_MEM, X_HEADS, X_HEAD_DIM)
    return (xp.reshape(batch, seq, D_MODEL), xs.reshape(dec_batch, dec_seq, D_MODEL),
            mk_p.reshape(kv_shape), mv_p.reshape(kv_shape),
            tails(cb_p, B_HPAD, B_KERNEL - 1), tails(cc_p, C_HPAD, C_KERNEL - 1), tails(pd_p, D_HPAD, D_HIST),
            jnp.stack(av_s),
            tails(cb_s, B_HPAD, B_KERNEL - 1), tails(cc_s, C_HPAD, C_KERNEL - 1), tails(pd_s, D_HPAD, D_HIST))
```

```python
import functools

import jax
import jax.numpy as jnp
from jax import lax
from jax.experimental import pallas as pl
from jax.experimental.pallas import tpu as pltpu

D_MODEL = 1024
DEPTH = 4
CHUNK = 64
PAST_LEN = 2048
BR = D_MODEL // 4
N_BRANCH = 5
A_BLOCK = 128
A_GROUPS = 4
B_KERNEL = 31
C_KERNEL = 3
D_WINDOWS = (2, 4, 8, 16)
D_HIST = max(D_WINDOWS) - 1
N_MEM = 256
X_HEADS = 4
X_HEAD_DIM = BR // X_HEADS
D_FF = ((8 * D_MODEL // 3 + 255) // 256) * 256
W_IN_COLS = 9 * BR
EPS = 1e-6

SUBLANES = 8
LANES = 128
B_HPAD = 32
C_HPAD = 8
D_HPAD = 16
VMEM_LIMIT = 56 * 1024 * 1024
MIXER_SUBTILES = 1
FF_CHUNK = 1024
ATTN_ROWS = 512

BF16 = jnp.bfloat16
F32 = jnp.float32

V256_A_LN_G, V256_A_LN_B, V256_B_CB, V256_B_LN_G, V256_B_LN_B, V256_D_SCALE, V256_C_W0 = range(7)
V1024_NORM1, V1024_BGATE0 = 0, 1


def _dot(a, b):
    return jnp.dot(a, b, preferred_element_type=F32)


def _sigmoid(x):
    return 0.5 * jnp.tanh(0.5 * x) + 0.5


def _rms(x, g):
    return x * lax.rsqrt(jnp.mean(x * x, axis=-1, keepdims=True) + EPS) * g


def _layernorm(x, g, b):
    mu = jnp.mean(x, axis=-1, keepdims=True)
    xc = x - mu
    var = jnp.mean(xc * xc, axis=-1, keepdims=True)
    return xc * lax.rsqrt(var + EPS) * g + b


def _lane_group_select(lane_group, vals):
    out = vals[-1]
    for g in range(len(vals) - 2, -1, -1):
        out = jnp.where(lane_group == g, vals[g], out)
    return out


def _shift_up(x, n):
    return x if n == 0 else pltpu.roll(x, x.shape[0] - n, axis=0)


def _shift_down(x, n):
    return x if n == 0 else pltpu.roll(x, n, axis=0)


def _causal_conv31(prev, cur, w_ref, bias):
    ch = cur.shape[0]
    ext = jnp.concatenate([prev, cur], axis=0)
    lead = B_HPAD - (B_KERNEL - 1)
    acc = bias + w_ref(B_KERNEL - 1) * cur
    for b in range(SUBLANES):
        shifted = _shift_up(ext, b)
        for a in range(B_HPAD // SUBLANES):
            k = a * SUBLANES + b - lead
            if 0 <= k < B_KERNEL - 1:
                acc = acc + w_ref(k) * shifted[a * SUBLANES:a * SUBLANES + ch]
    return acc


def _mixer_kernel(*refs, n_seg, seg_len, has_state, pos0, emit_va, row_chunk, n_sub):
    it = iter(refs)
    x_ref, k_ref, v_ref = next(it), next(it), next(it)
    if has_state:
        hb_ref, hc_ref, hd_ref = next(it), next(it), next(it)
    v1024_ref, w_in_ref, v256_ref, aw_ref, abias_ref = (next(it) for _ in range(5))
    bcw_ref, dwbd_ref, wbr_ref, wg_ref, wout_ref = (next(it) for _ in range(5))
    x1_ref, hbo_ref, hco_ref, hdo_ref = (next(it) for _ in range(4))
    vao_ref = next(it) if emit_va else None
    xnb_ref, z_ref, hist_b, hist_c, hist_d, br_ref, ua_ref, va_ref = (next(it) for _ in range(8))

    rows = n_seg * seg_len
    j = pl.program_id(1)

    def v256(r, lanes=slice(None)):
        return v256_ref[0, r:r + 1, lanes]

    @pl.when(j == 0)
    def _():
        if has_state:
            hist_b[...] = hb_ref[...]
            hist_c[...] = hc_ref[...]
            hist_d[...] = hd_ref[...]
        else:
            hist_b[...] = jnp.zeros(hist_b.shape, F32)
            hist_c[...] = jnp.zeros(hist_c.shape, F32)
            hist_d[...] = jnp.zeros(hist_d.shape, F32)

    lane_group = lax.broadcasted_iota(jnp.int32, (1, BR), 1) // (BR // 4)
    d_window = _lane_group_select(lane_group, [jnp.full((1, BR), w, jnp.int32) for w in D_WINDOWS])
    half_lo = lax.broadcasted_iota(jnp.int32, (1, LANES), 1) < LANES // 2

    ch = row_chunk

    def chunk_task(s, c, prev_b, prev_c, prev_d):
        t0 = c * ch
        rs = pl.ds(s * seg_len + t0, ch)

        ga = jax.nn.gelu(z_ref[rs, 0:2 * BR])
        ua_ref[rs, :] = ga[:, :BR]
        va = _layernorm(ga[:, BR:], v256(V256_A_LN_G), v256(V256_A_LN_B))
        va_ref[rs, :] = va
        if emit_va:
            vao_ref[rs, :] = va

        zb = z_ref[rs, 2 * BR:4 * BR]
        b_in = zb[:, :BR] * _sigmoid(zb[:, BR:])
        conv = []
        for h in range(BR // LANES):
            ls = slice(h * LANES, (h + 1) * LANES)
            conv.append(_causal_conv31(prev_b[:, ls], b_in[:, ls],
                                       lambda k, ls=ls: bcw_ref[0, k:k + 1, ls], v256(V256_B_CB, ls)))
        lnb = _layernorm(jnp.concatenate(conv, axis=1), v256(V256_B_LN_G), v256(V256_B_LN_B))
        br_ref[1, rs, :] = (lnb * _sigmoid(lnb)).astype(BF16)
        prev_b = jnp.concatenate([prev_b, b_in], axis=0)[-B_HPAD:]

        zc = z_ref[rs, 4 * BR:7 * BR]
        cc = zc[:, BR:2 * BR] * zc[:, 2 * BR:]
        ext_c = jnp.concatenate([prev_c, cc], axis=0)
        conv_c = v256(V256_C_W0 + C_KERNEL - 1) * cc
        for k in range(C_KERNEL - 1):
            off = C_HPAD - (C_KERNEL - 1) + k
            conv_c = conv_c + v256(V256_C_W0 + k) * _shift_up(ext_c, off)[:ch]
        br_ref[2, rs, :] = (zc[:, :BR] * conv_c).astype(BF16)
        prev_c = ext_c[-C_HPAD:]

        zd = z_ref[rs, 7 * BR:8 * BR]
        ext_d = jnp.concatenate([prev_d, zd], axis=0)
        wins = []
        for h in range(BR // LANES):
            e = ext_d[:, h * LANES:(h + 1) * LANES]
            s2 = e + _shift_down(e, 1)
            s4 = s2 + _shift_down(s2, 2)
            if h == 0:
                lo, hi = s2[D_HPAD:], s4[D_HPAD:]
            else:
                s8 = s4 + _shift_down(s4, 4)
                lo, hi = s8[D_HPAD:], s8[D_HPAD:] + s8[D_HPAD - 8:-8]
            wins.append(jnp.where(half_lo, lo, hi))
        win = jnp.concatenate(wins, axis=1)
        pos1 = pos0 + j * seg_len + t0 + 1 + lax.broadcasted_iota(jnp.int32, (ch, 1), 0)
        cnt = jnp.minimum(d_window, pos1).astype(F32)
        br_ref[3, rs, :] = (win / cnt - zd).astype(BF16)
        return prev_b, prev_c, ext_d[-D_HPAD:]

    blk_seg = min(A_BLOCK, seg_len)
    ri = lax.broadcasted_iota(jnp.int32, (A_BLOCK, A_BLOCK), 0)
    ci = lax.broadcasted_iota(jnp.int32, (A_BLOCK, A_BLOCK), 1)
    keep = (ri // blk_seg == ci // blk_seg) & ((ci % blk_seg) // CHUNK <= (ri % blk_seg) // CHUNK)
    wsp = jnp.concatenate([jnp.where(keep, aw_ref[0, g], 0.0).astype(BF16) for g in range(A_GROUPS)], axis=0)

    def spatial_task(blk):
        rs = pl.ds(blk * A_BLOCK, A_BLOCK)
        sp4 = _dot(wsp, va_ref[rs, :].astype(BF16))
        sp = _lane_group_select(lane_group, [sp4[g * A_BLOCK:(g + 1) * A_BLOCK] for g in range(A_GROUPS)])
        br_ref[0, rs, :] = (ua_ref[rs, :] * (sp + abias_ref[0])).astype(BF16)

    def attn_task(s, r0, nr):
        rs = pl.ds(r0, nr)
        q = z_ref[rs, 8 * BR:9 * BR] * (X_HEAD_DIM ** -0.5)
        q4 = jnp.concatenate([jnp.where(lane_group == h, q, 0.0) for h in range(X_HEADS)], axis=0)
        sc = lax.dot_general(q4.astype(BF16), k_ref[s], (((1,), (1,)), ((), ())),
                             preferred_element_type=F32)
        p = jnp.exp(sc - jnp.max(sc, axis=-1, keepdims=True))
        inv = 1.0 / jnp.sum(p, axis=-1, keepdims=True)
        o4 = _dot(p.astype(BF16), v_ref[s]) * inv
        o = _lane_group_select(lane_group, [o4[h * nr:(h + 1) * nr] for h in range(X_HEADS)])
        br_ref[4, rs, :] = o.astype(BF16)

    sub_rows = rows // n_sub
    next_blk = 0
    hist = None
    for sub in range(n_sub):
        rsub = pl.ds(sub * sub_rows, sub_rows)

        xnb_ref[rsub, :] = _rms(x_ref[rsub, :], v1024_ref[0, V1024_NORM1:V1024_NORM1 + 1, :]).astype(BF16)
        z_ref[rsub, :] = _dot(xnb_ref[rsub, :], w_in_ref[0])

        if n_seg == 1:
            chunk_ids = [(0, c) for c in range(sub * sub_rows // ch, (sub + 1) * sub_rows // ch)]
            ar = min(ATTN_ROWS, sub_rows)
            attn_ids = [(0, r0, ar) for r0 in range(sub * sub_rows, (sub + 1) * sub_rows, ar)]
        else:
            chunk_ids = [(s, c) for s in range(n_seg) for c in range(seg_len // ch)]
            attn_ids = [(s, s * seg_len, seg_len) for s in range(n_seg)]
        slots = len(chunk_ids)
        for i, (s, c) in enumerate(chunk_ids):
            if c == 0:
                hist = (hist_b[s], hist_c[s], hist_d[s])
            hist = chunk_task(s, c, *hist)
            if c == seg_len // ch - 1:
                hist_b[s], hist_c[s], hist_d[s] = hist
            while (next_blk + 1) * A_BLOCK <= s * seg_len + (c + 1) * ch:
                spatial_task(next_blk)
                next_blk += 1
            for ids in attn_ids[i * len(attn_ids) // slots:(i + 1) * len(attn_ids) // slots]:
                attn_task(*ids)

        merged = None
        for n in range(N_BRANCH):
            bn = br_ref[n, rsub, :]
            if n == 3:
                bn = (_dot(bn, dwbd_ref[0]) * v256(V256_D_SCALE)).astype(BF16)
            gate = _sigmoid(_dot(xnb_ref[rsub, :], wg_ref[0, n])
                            + v1024_ref[0, V1024_BGATE0 + n:V1024_BGATE0 + n + 1, :])
            term = gate * _dot(bn, wbr_ref[0, n])
            merged = term if merged is None else merged + term
        x1_ref[rsub, :] = x_ref[rsub, :] + _dot(merged.astype(BF16), wout_ref[0])

    hbo_ref[...] = hist_b[...]
    hco_ref[...] = hist_c[...]
    hdo_ref[...] = hist_d[...]


def _resident(shape, layer):
    nd = len(shape)
    return pl.BlockSpec((1,) + tuple(shape[1:]), lambda *_: (layer,) + (0,) * (nd - 1),
                        pipeline_mode=pl.Buffered(1))


def _mixer(x, k_all, v_all, states, lw, layer, *, n_seq, seq_len, n_seg, seg_len, pos0, emit_va):
    has_state = states is not None
    rows = n_seg * seg_len
    tiles = seq_len // seg_len
    assert seq_len % seg_len == 0 and n_seq % n_seg == 0 and rows % A_BLOCK == 0
    assert tiles == 1 or n_seg == 1
    assert seg_len % A_BLOCK == 0 or A_BLOCK % seg_len == 0
    assert seg_len >= B_HPAD
    row_chunk = min(64, seg_len)
    n_sub = MIXER_SUBTILES if n_seg == 1 and seg_len % (MIXER_SUBTILES * A_BLOCK) == 0 else 1
    grid = (n_seq // n_seg, tiles)
    per_layer = n_seq // n_seg

    in_specs = [
        pl.BlockSpec((rows, D_MODEL), lambda b, j: (b * tiles + j, 0)),
        pl.BlockSpec((n_seg, N_MEM, BR), lambda b, j: (layer * per_layer + b, 0, 0)),
        pl.BlockSpec((n_seg, N_MEM, BR), lambda b, j: (layer * per_layer + b, 0, 0)),
    ]
    args = [x, k_all, v_all]
    if has_state:
        for st, hpad in zip(states, (B_HPAD, C_HPAD, D_HPAD)):
            in_specs.append(pl.BlockSpec((n_seg, hpad, BR), lambda b, j: (layer * per_layer + b, 0, 0)))
            args.append(st)
    names = ("v1024", "w_in", "v256", "a_w", "a_bias", "b_conv_w", "d_w_bd", "w_branch", "w_gate", "w_out")
    for name in names:
        in_specs.append(_resident(lw[name].shape, layer))
        args.append(lw[name])

    out_shape = [
        jax.ShapeDtypeStruct((n_seq * seq_len, D_MODEL), F32),
        jax.ShapeDtypeStruct((n_seq, B_HPAD, BR), F32),
        jax.ShapeDtypeStruct((n_seq, C_HPAD, BR), F32),
        jax.ShapeDtypeStruct((n_seq, D_HPAD, BR), F32),
    ]
    out_specs = [
        pl.BlockSpec((rows, D_MODEL), lambda b, j: (b * tiles + j, 0)),
        pl.BlockSpec((n_seg, B_HPAD, BR), lambda b, j: (b, 0, 0)),
        pl.BlockSpec((n_seg, C_HPAD, BR), lambda b, j: (b, 0, 0)),
        pl.BlockSpec((n_seg, D_HPAD, BR), lambda b, j: (b, 0, 0)),
    ]
    if emit_va:
        out_shape.append(jax.ShapeDtypeStruct((n_seq * seq_len, BR), F32))
        out_specs.append(pl.BlockSpec((rows, BR), lambda b, j: (b * tiles + j, 0)))

    scratch = [
        pltpu.VMEM((rows, D_MODEL), BF16),
        pltpu.VMEM((rows, W_IN_COLS), F32),
        pltpu.VMEM((n_seg, B_HPAD, BR), F32),
        pltpu.VMEM((n_seg, C_HPAD, BR), F32),
        pltpu.VMEM((n_seg, D_HPAD, BR), F32),
        pltpu.VMEM((N_BRANCH, rows, BR), BF16),
        pltpu.VMEM((rows, BR), F32),
        pltpu.VMEM((rows, BR), F32),
    ]
    kern = functools.partial(_mixer_kernel, n_seg=n_seg, seg_len=seg_len, has_state=has_state,
                             pos0=pos0, emit_va=emit_va, row_chunk=row_chunk, n_sub=n_sub)
    return pl.pallas_call(
        kern, grid=grid, in_specs=in_specs, out_specs=out_specs, out_shape=out_shape,
        scratch_shapes=scratch, name="mixer_state" if has_state else "mixer_prompt",
        compiler_params=pltpu.CompilerParams(dimension_semantics=("arbitrary", "arbitrary"),
                                             vmem_limit_bytes=VMEM_LIMIT),
    )(*args)


def _ffn_kernel(x_ref, g_ref, wg_ref, wu_ref, wd_ref, gf_ref, o_ref, *, final_norm):
    x = x_ref[...]
    hn = _rms(x, g_ref[0]).astype(BF16)
    y = x
    for c0 in range(0, D_FF, FF_CHUNK):
        c1 = min(c0 + FF_CHUNK, D_FF)
        hg = _dot(hn, wg_ref[0, :, c0:c1])
        act = (hg * _sigmoid(hg) * _dot(hn, wu_ref[0, :, c0:c1])).astype(BF16)
        y = y + _dot(act, wd_ref[0, c0:c1, :])
    if final_norm:
        y = _rms(y, gf_ref[...])
    o_ref[...] = y


def _ffn(x, lw, layer, final_g, *, row_tile, final_norm):
    n_rows = x.shape[0]
    assert n_rows % row_tile == 0
    names = ("norm2_g", "w_ffn_gate", "w_ffn_up", "w_ffn_down")
    in_specs = [pl.BlockSpec((row_tile, D_MODEL), lambda i: (i, 0))]
    in_specs += [_resident(lw[n].shape, layer) for n in names]
    in_specs.append(pl.BlockSpec((1, D_MODEL), lambda i: (0, 0)))
    return pl.pallas_call(
        functools.partial(_ffn_kernel, final_norm=final_norm),
        grid=(n_rows // row_tile,), in_specs=in_specs,
        out_specs=pl.BlockSpec((row_tile, D_MODEL), lambda i: (i, 0)),
        out_shape=jax.ShapeDtypeStruct((n_rows, D_MODEL), F32), name="ffn",
        compiler_params=pltpu.CompilerParams(dimension_semantics=("arbitrary",),
                                             vmem_limit_bytes=VMEM_LIMIT),
    )(x, *[lw[n] for n in names], final_g)


def _memkv_kernel(m_ref, g_ref, w_ref, k_ref, v_ref, kb_ref, vb_ref):
    kv = _dot(_rms(m_ref[...], g_ref[0]).astype(BF16), w_ref[0])
    k_ref[0] = kv[:, :BR]
    v_ref[0] = kv[:, BR:]
    kb_ref[0] = kv[:, :BR].astype(BF16)
    vb_ref[0] = kv[:, BR:].astype(BF16)


def _memory_kv(mem_rows, mem_norm_g, w_mem_kv, *, row_tile):
    n_rows = mem_rows.shape[0]
    assert n_rows % row_tile == 0
    tiles = n_rows // row_tile
    out = jax.ShapeDtypeStruct((DEPTH, n_rows, BR), F32)
    out_b = jax.ShapeDtypeStruct((DEPTH, n_rows, BR), BF16)
    return pl.pallas_call(
        _memkv_kernel, grid=(DEPTH, tiles),
        in_specs=[pl.BlockSpec((row_tile, D_MODEL), lambda l, i: (i, 0)),
                  pl.BlockSpec((1, 1, D_MODEL), lambda l, i: (l, 0, 0)),
                  pl.BlockSpec((1, D_MODEL, 2 * BR), lambda l, i: (l, 0, 0))],
        out_specs=[pl.BlockSpec((1, row_tile, BR), lambda l, i: (l, i, 0))] * 4,
        out_shape=[out, out, out_b, out_b], name="memory_kv",
        compiler_params=pltpu.CompilerParams(dimension_semantics=("arbitrary", "arbitrary"),
                                             vmem_limit_bytes=VMEM_LIMIT),
    )(mem_rows, mem_norm_g, w_mem_kv)


MIXER_ROWS = 512
FFN_ROWS = 1024


def _mixer_prompt_call(x, k_all, v_all, lw, layer, batch, seq):
    return _mixer(x, k_all, v_all, None, lw, layer, n_seq=batch, seq_len=seq, n_seg=1,
                  seg_len=min(MIXER_ROWS, seq), pos0=0, emit_va=False)


def _mixer_state_call(x, k_all, v_all, states, lw, layer, dec_batch, dec_seq):
    return _mixer(x, k_all, v_all, states, lw, layer, n_seq=dec_batch, seq_len=dec_seq,
                  n_seg=8, seg_len=dec_seq, pos0=PAST_LEN, emit_va=True)


def _ffn_call(x, lw, layer, final_g, final_norm):
    return _ffn(x, lw, layer, final_g, row_tile=min(FFN_ROWS, x.shape[0]), final_norm=final_norm)


def _pad_rows_front(a, total):
    pad = total - a.shape[-2]
    return jnp.pad(a, [(0, 0)] * (a.ndim - 2) + [(pad, 0), (0, 0)])


def _spatial_layout(a_ws, a_bs, blk_seg):
    reps = A_BLOCK // blk_seg
    w = jnp.tile(a_ws[:, :, :blk_seg, :blk_seg], (1, 1, reps, reps))
    bias = jnp.repeat(jnp.swapaxes(a_bs, 1, 2), BR // A_GROUPS, axis=2)
    bias = jnp.tile(bias[:, :blk_seg, :], (1, reps, 1))
    return w, bias


def _layout_states(state_conv_b, state_conv_c, state_pool_d):
    return tuple(_pad_rows_front(st, hpad).reshape(-1, hpad, BR)
                 for st, hpad in ((state_conv_b, B_HPAD), (state_conv_c, C_HPAD), (state_pool_d, D_HPAD)))


def _layout_params(seq, dec_seq, norm1_g, w_in, a_ln_g, a_ln_b, a_ws, a_bs, b_conv_w, b_conv_b, b_ln_g,
                   b_ln_b, c_conv_w, d_w, d_scale, w_branch, w_gate, b_gate, w_out, norm2_g, w_ffn_gate,
                   w_ffn_up, w_ffn_down):
    depth = w_in.shape[0]
    zeros256 = jnp.zeros((depth, 16 - 6 - C_KERNEL, BR), F32)
    v256 = jnp.concatenate([a_ln_g[:, None], a_ln_b[:, None], b_conv_b[:, None], b_ln_g[:, None],
                            b_ln_b[:, None], d_scale[:, None], c_conv_w, zeros256], axis=1)
    v1024 = jnp.concatenate([norm1_g[:, None], b_gate, jnp.zeros((depth, 2, D_MODEL), F32)], axis=1)
    eye = jnp.eye(len(D_WINDOWS), dtype=F32)
    d_w_bd = jnp.einsum('lgcd,gh->lgchd', d_w, eye).reshape(depth, BR, BR).astype(BF16)
    lw = {
        "v1024": v1024, "v256": v256, "w_in": w_in.astype(BF16), "b_conv_w": b_conv_w,
        "d_w_bd": d_w_bd, "w_branch": w_branch.astype(BF16), "w_gate": w_gate.astype(BF16),
        "w_out": w_out.astype(BF16), "norm2_g": norm2_g[:, None],
        "w_ffn_gate": w_ffn_gate.astype(BF16), "w_ffn_up": w_ffn_up.astype(BF16),
        "w_ffn_down": w_ffn_down.astype(BF16),
    }
    lw_p = dict(lw)
    lw_p["a_w"], lw_p["a_bias"] = _spatial_layout(a_ws, a_bs, min(A_BLOCK, seq))
    lw_s = dict(lw)
    lw_s["a_w"], lw_s["a_bias"] = _spatial_layout(a_ws, a_bs, min(A_BLOCK, dec_seq))
    return lw, lw_p, lw_s


def kernel(x_prompt, x_sample, mem_prompt, cache_mem_k, cache_mem_v, state_conv_b, state_conv_c, state_pool_d, norm1_g, mem_norm_g, w_in, a_ln_g, a_ln_b, a_ws, a_bs, b_conv_w, b_conv_b, b_ln_g, b_ln_b, c_conv_w, d_w, d_scale, w_mem_kv, w_branch, w_gate, b_gate, w_out, norm2_g, w_ffn_gate, w_ffn_up, w_ffn_down, final_norm_g):
    batch, seq, _ = x_prompt.shape
    dec_batch, dec_seq, _ = x_sample.shape
    depth = w_in.shape[0]
    assert depth == DEPTH
    lw, lw_p, lw_s = _layout_params(
        seq, dec_seq, norm1_g, w_in, a_ln_g, a_ln_b, a_ws, a_bs, b_conv_w, b_conv_b, b_ln_g, b_ln_b,
        c_conv_w, d_w, d_scale, w_branch, w_gate, b_gate, w_out, norm2_g, w_ffn_gate, w_ffn_up, w_ffn_down)
    final_g = final_norm_g[None]

    mk_p, mv_p, kb_p, vb_p = _memory_kv(mem_prompt.reshape(batch * N_MEM, D_MODEL), mem_norm_g[:, None],
                                        w_mem_kv.astype(BF16), row_tile=1024)
    k_p = kb_p.reshape(depth * batch, N_MEM, BR)
    v_p = vb_p.reshape(depth * batch, N_MEM, BR)
    k_s = cache_mem_k.reshape(depth * dec_batch, N_MEM, BR).astype(BF16)
    v_s = cache_mem_v.reshape(depth * dec_batch, N_MEM, BR).astype(BF16)
    states = _layout_states(state_conv_b, state_conv_c, state_pool_d)

    xp = x_prompt.reshape(batch * seq, D_MODEL)
    xs = x_sample.reshape(dec_batch * dec_seq, D_MODEL)
    cb_p, cc_p, pd_p, av_s, cb_s, cc_s, pd_s = [], [], [], [], [], [], []
    for l in range(depth):
        last = l == depth - 1
        xp, hb, hc, hd = _mixer_prompt_call(xp, k_p, v_p, lw_p, l, batch, seq)
        xp = _ffn_call(xp, lw, l, final_g, last)
        cb_p.append(hb); cc_p.append(hc); pd_p.append(hd)
        xs, hb, hc, hd, va = _mixer_state_call(xs, k_s, v_s, states, lw_s, l, dec_batch, dec_seq)
        xs = _ffn_call(xs, lw, l, final_g, last)
        cb_s.append(hb); cc_s.append(hc); pd_s.append(hd)
        av_s.append(va.reshape(dec_batch, dec_seq, BR))

    def tails(parts, hpad, hist):
        return jnp.stack(parts)[:, :, hpad - hist:, :]

    kv_shape = (depth, batch, N_MEM, X_HEADS, X_HEAD_DIM)
    return (xp.reshape(batch, seq, D_MODEL), xs.reshape(dec_batch, dec_seq, D_MODEL),
            mk_p.reshape(kv_shape), mv_p.reshape(kv_shape),
            tails(cb_p, B_HPAD, B_KERNEL - 1), tails(cc_p, C_HPAD, C_KERNEL - 1), tails(pd_p, D_HPAD, D_HIST),
            jnp.stack(av_s),
            tails(cb_s, B_HPAD, B_KERNEL - 1), tails(cc_s, C_HPAD, C_KERNEL - 1), tails(pd_s, D_HPAD, D_HIST))
```
